```python
import math
import jax, jax.numpy as jnp
from jax import lax
import numpy as np


D_MODEL = 2048
BATCH = 8
SEQ = 4096
DEPTH = 1
DEC_BATCH = 16
DEC_SEQ = 2048
PAST_LEN = 128

MIX_WIDTH = D_MODEL
HEAD_DIM = 128
ATTN_WIDTH = MIX_WIDTH // 2
N_HEADS = ATTN_WIDTH // HEAD_DIM
N_KV = 2
GROUP = N_HEADS // N_KV
KV_WIDTH = N_KV * HEAD_DIM
WINDOW = 128
BLOCK = 128
ROPE_THETA = 10000.0
GMLP_WIDTH = MIX_WIDTH - ATTN_WIDTH
GMLP_HEAD_DIM = 128
N_GMLP_HEADS = GMLP_WIDTH // GMLP_HEAD_DIM
CHUNK = 128
PLE_DIM = 256
EPS = 1e-6
IN_SPLITS = (
    ATTN_WIDTH,
    ATTN_WIDTH + KV_WIDTH,
    ATTN_WIDTH + 2 * KV_WIDTH,
    2 * ATTN_WIDTH + 2 * KV_WIDTH,
    2 * ATTN_WIDTH + 2 * KV_WIDTH + GMLP_WIDTH,
    2 * ATTN_WIDTH + 2 * KV_WIDTH + 2 * GMLP_WIDTH,
)
IN_WIDTH = 2 * ATTN_WIDTH + 2 * KV_WIDTH + 3 * GMLP_WIDTH

kernel_name = "hymba_window_gqa_gmlp_sandwich_ple_encoder"


def rms_norm(x, g):
    xf = x.astype(jnp.float32)
    y = xf * lax.rsqrt(jnp.mean(xf * xf, axis=-1, keepdims=True) + EPS)
    return (y * g.astype(jnp.float32)).astype(x.dtype)


def rope(x):
    S = x.shape[1]
    inv = 1.0 / (ROPE_THETA ** (jnp.arange(0, HEAD_DIM, 2, dtype=jnp.float32) / HEAD_DIM))
    ang = jnp.arange(S, dtype=jnp.float32)[:, None] * inv[None, :]
    cos = jnp.cos(ang)[None, :, None, :]
    sin = jnp.sin(ang)[None, :, None, :]
    xf = x.astype(jnp.float32)
    x1, x2 = jnp.split(xf, 2, axis=-1)
    out = jnp.concatenate([x1 * cos - x2 * sin, x2 * cos + x1 * sin], axis=-1)
    return out.astype(x.dtype)


def window_attention(q, k, v, sink):
    B, S, H, D = q.shape
    nb = S // BLOCK
    qb = q.reshape(B, nb, BLOCK, N_KV, GROUP, D)
    pad = ((0, 0), (BLOCK, BLOCK), (0, 0), (0, 0))

    def band(t):
        tb = jnp.pad(t, pad).reshape(B, nb + 2, BLOCK, N_KV, D)
        return jnp.concatenate([tb[:, :-2], tb[:, 1:-1], tb[:, 2:]], axis=2)

    kb, vb = band(k), band(v)
    s = jnp.einsum('bnqgrd,bnkgd->bngrqk', qb, kb).astype(jnp.float32) * (D ** -0.5)
    qpos = jnp.arange(nb)[:, None, None] * BLOCK + jnp.arange(BLOCK)[None, :, None]
    kpos = jnp.arange(nb)[:, None, None] * BLOCK - BLOCK + jnp.arange(3 * BLOCK)[None, None, :]
    valid = (jnp.abs(qpos - kpos) <= WINDOW) & (kpos >= 0) & (kpos < S)
    s = jnp.where(valid[None, :, None, None], s, -1e30)
    sink_logit = jnp.broadcast_to(
        sink.astype(jnp.float32).reshape(1, 1, N_KV, GROUP, 1, 1), s.shape[:-1] + (1,))
    probs = jax.nn.softmax(jnp.concatenate([s, sink_logit], axis=-1), axis=-1)[..., :-1]
    o = jnp.einsum('bngrqk,bnkgd->bnqgrd', probs.astype(v.dtype), vb)
    return o.reshape(B, S, H * D)


def spatial_gating(u, v, ln_g, ln_b, ws, bs):
    B, S, _ = v.shape
    nc = S // CHUNK
    vf = v.astype(jnp.float32)
    mu = jnp.mean(vf, axis=-1, keepdims=True)
    var = jnp.mean(jnp.square(vf - mu), axis=-1, keepdims=True)
    vn = ((vf - mu) * lax.rsqrt(var + EPS) * ln_g.astype(jnp.float32)
          + ln_b.astype(jnp.float32)).astype(v.dtype)
    vc = vn.reshape(B, nc, CHUNK, N_GMLP_HEADS, GMLP_HEAD_DIM)
    mixed = jnp.einsum('hpq,bnqhc->bnphc', ws, vc) + bs.T[None, None, :, :, None]
    return u * mixed.reshape(B, S, GMLP_WIDTH)


def layer(x, p, pre_g, w_in, sink, ln_g, ln_b, ws, bs, w_out, post_g, w_pe, w_pg):
    B, S, _ = x.shape
    h = rms_norm(x, pre_g)
    z = h @ w_in
    q, k, v, g_attn, u, vg, g_gmlp = jnp.split(z, IN_SPLITS, axis=-1)
    q = rope(q.reshape(B, S, N_HEADS, HEAD_DIM))
    k = rope(k.reshape(B, S, N_KV, HEAD_DIM))
    v = v.reshape(B, S, N_KV, HEAD_DIM)
    a = window_attention(q, k, v, sink) * jax.nn.silu(g_attn)
    m = spatial_gating(jax.nn.gelu(u, approximate=False), jax.nn.gelu(vg, approximate=False),
                       ln_g, ln_b, ws, bs) * jax.nn.silu(g_gmlp)
    y = jnp.concatenate([a, m], axis=-1) @ w_out
    x = x + rms_norm(y, post_g)
    x = x + jax.nn.sigmoid(x @ w_pg) * (p @ w_pe)
    return x


def setup_inputs(seed: int = 0) -> dict:
    key = jax.random.key(seed)
    ks = jax.random.split(key, 16)
    f = jnp.float32
    nrm = jax.random.normal
    return {
        "x_prompt": nrm(ks[0], (BATCH, SEQ, D_MODEL), f),
        "x_sample": nrm(ks[1], (DEC_BATCH, DEC_SEQ, D_MODEL), f),
        "p_prompt": nrm(ks[2], (DEPTH, BATCH, SEQ, PLE_DIM), f),
        "p_sample": nrm(ks[3], (DEPTH, DEC_BATCH, DEC_SEQ, PLE_DIM), f),
        "pre_norm_g": 1.0 + 0.05 * nrm(ks[4], (DEPTH, D_MODEL), f),
        "w_in": nrm(ks[5], (DEPTH, D_MODEL, IN_WIDTH), f) * D_MODEL ** -0.5,
        "attn_sink": 0.5 * nrm(ks[6], (DEPTH, N_HEADS), f),
        "gmlp_ln_g": 1.0 + 0.05 * nrm(ks[7], (DEPTH, GMLP_WIDTH), f),
        "gmlp_ln_b": 0.02 * nrm(ks[8], (DEPTH, GMLP_WIDTH), f),
        "gmlp_ws": nrm(ks[9], (DEPTH, N_GMLP_HEADS, CHUNK, CHUNK), f) * CHUNK ** -0.5,
        "gmlp_bs": 1.0 + 0.05 * nrm(ks[10], (DEPTH, N_GMLP_HEADS, CHUNK), f),
        "w_out": nrm(ks[11], (DEPTH, MIX_WIDTH, D_MODEL), f) * MIX_WIDTH ** -0.5,
        "post_norm_g": 1.0 + 0.05 * nrm(ks[12], (DEPTH, D_MODEL), f),
        "w_pe": nrm(ks[13], (DEPTH, PLE_DIM, D_MODEL), f) * PLE_DIM ** -0.5,
        "w_pg": nrm(ks[14], (DEPTH, D_MODEL, D_MODEL), f) * D_MODEL ** -0.5,
    }


def reference(x_prompt, x_sample, p_prompt, p_sample, pre_norm_g, w_in, attn_sink,
              gmlp_ln_g, gmlp_ln_b, gmlp_ws, gmlp_bs, w_out, post_norm_g, w_pe, w_pg):
    y_prompt = x_prompt
    y_sample = x_sample
    for i in range(DEPTH):
        params = (pre_norm_g[i], w_in[i], attn_sink[i], gmlp_ln_g[i], gmlp_ln_b[i],
                  gmlp_ws[i], gmlp_bs[i], w_out[i], post_norm_g[i], w_pe[i], w_pg[i])
        y_prompt = layer(y_prompt, p_prompt[i], *params)
        y_sample = layer(y_sample, p_sample[i], *params)
    return (y_prompt, y_sample)
```

```python
import functools
import math

import jax
import jax.numpy as jnp
from jax import lax
from jax.experimental import pallas as pl
from jax.experimental.pallas import tpu as pltpu

D_MODEL = 2048
HEAD_DIM = 128
ATTN_WIDTH = 1024
N_HEADS = 8
N_KV = 2
GROUP = 4
KV_WIDTH = 256
WINDOW = 128
BLOCK = 128
ROPE_THETA = 10000.0
GMLP_WIDTH = 1024
N_GMLP_HEADS = 8
CHUNK = 128
PLE_DIM = 256
EPS = 1e-6
IN_WIDTH = 5632
NEG_INF = -1e30

VMEM_LIMIT_BYTES = 56 * 1024 * 1024

TM_IN = 512
NCHUNK = 512
TQ = 512
TM_OUT = 256

_BF16 = jnp.bfloat16
_F32 = jnp.float32


def _sigmoid(x):
    return 1.0 / (1.0 + jnp.exp(-x))


def _gelu_exact(x):
    return 0.5 * x * (1.0 + lax.erf(x * (2.0 ** -0.5)))


def _in_proj_kernel(x_ref, g_ref, w_ref, cos_ref, sin_ref,
                    q_ref, k_ref, v_ref, ga_ref, u_ref, vg_ref, gg_ref, h_ref):
    tm = x_ref.shape[0]
    rows = 64
    for r in range(tm // rows):
        x = x_ref[r * rows:(r + 1) * rows, :]
        ms = jnp.mean(x * x, axis=-1, keepdims=True)
        h = x * lax.rsqrt(ms + EPS) * g_ref[...]
        h_ref[r * rows:(r + 1) * rows, :] = h.astype(_BF16)

    cos = cos_ref[...]
    sin = sin_ref[...]

    def rope(t):
        return t * cos + pltpu.roll(t, HEAD_DIM // 2, 1) * sin

    def proj(c):
        return jnp.dot(h_ref[...], w_ref[:, c * NCHUNK:(c + 1) * NCHUNK],
                       preferred_element_type=_F32)

    heads_per_chunk = NCHUNK // HEAD_DIM
    for c in range(2):
        acc = proj(c)
        for hd in range(heads_per_chunk):
            col = c * NCHUNK + hd * HEAD_DIM
            q_ref[:, col:col + HEAD_DIM] = rope(
                acc[:, hd * HEAD_DIM:(hd + 1) * HEAD_DIM]).astype(_BF16)
    acc = proj(2)
    for hd in range(N_KV):
        k_ref[:, hd * HEAD_DIM:(hd + 1) * HEAD_DIM] = rope(
            acc[:, hd * HEAD_DIM:(hd + 1) * HEAD_DIM]).astype(_BF16)
    v_ref[...] = acc[:, KV_WIDTH:].astype(_BF16)
    for c in range(2):
        acc = proj(3 + c)
        ga_ref[:, c * NCHUNK:(c + 1) * NCHUNK] = (acc * _sigmoid(acc)).astype(_BF16)
    for c in range(2):
        acc = proj(5 + c)
        u_ref[:, c * NCHUNK:(c + 1) * NCHUNK] = _gelu_exact(acc).astype(_BF16)
    for c in range(2):
        acc = proj(7 + c)
        vg_ref[:, c * NCHUNK:(c + 1) * NCHUNK] = _gelu_exact(acc).astype(_BF16)
    for c in range(2):
        acc = proj(9 + c)
        gg_ref[:, c * NCHUNK:(c + 1) * NCHUNK] = (acc * _sigmoid(acc)).astype(_BF16)


def _in_proj(x2d, seq, pre_g, w_in_bf, cos_t, sin_t):
    m = x2d.shape[0]
    tm = TM_IN
    assert m % tm == 0 and seq % tm == 0
    blocks_per_seq = seq // tm
    const = lambda i: (0, 0)
    row = lambda i: (i, 0)
    out_widths = (ATTN_WIDTH, KV_WIDTH, KV_WIDTH, ATTN_WIDTH,
                  GMLP_WIDTH, GMLP_WIDTH, GMLP_WIDTH)
    return pl.pallas_call(
        _in_proj_kernel,
        grid=(m // tm,),
        in_specs=[
            pl.BlockSpec((tm, D_MODEL), row),
            pl.BlockSpec((1, D_MODEL), const),
            pl.BlockSpec((D_MODEL, IN_WIDTH), const, pipeline_mode=pl.Buffered(1)),
            pl.BlockSpec((tm, HEAD_DIM), lambda i: (i % blocks_per_seq, 0)),
            pl.BlockSpec((tm, HEAD_DIM), lambda i: (i % blocks_per_seq, 0)),
        ],
        out_specs=[pl.BlockSpec((tm, w), row) for w in out_widths],
        out_shape=[jax.ShapeDtypeStruct((m, w), _BF16) for w in out_widths],
        scratch_shapes=[pltpu.VMEM((tm, D_MODEL), _BF16)],
        compiler_params=pltpu.CompilerParams(
            dimension_semantics=("arbitrary",),
            vmem_limit_bytes=VMEM_LIMIT_BYTES),
        name="in_proj",
    )(x2d, pre_g, w_in_bf, cos_t, sin_t)


def _attn_kernel(seq, sink_ref, q_ref, kp_ref, kc_ref, kn_ref,
                 vp_ref, vc_ref, vn_ref, ga_ref, o_ref):
    n = pl.program_id(1)
    tq = q_ref.shape[0]
    nsub = tq // BLOCK
    scale = HEAD_DIM ** -0.5

    iq = lax.broadcasted_iota(jnp.int32, (BLOCK, 3 * BLOCK), 0)
    ik = lax.broadcasted_iota(jnp.int32, (BLOCK, 3 * BLOCK), 1)
    band = jnp.abs(iq + BLOCK - ik) <= WINDOW
    ik_row = lax.broadcasted_iota(jnp.int32, (1, 3 * BLOCK), 1)

    for j in range(nsub):
        kbase = n * tq + (j - 1) * BLOCK
        in_seq = (ik_row + kbase >= 0) & (ik_row + kbase < seq)
        bias = jnp.where(band & in_seq, 0.0, NEG_INF).astype(_F32)
        bias4 = jnp.concatenate([bias] * GROUP, axis=0)
        for g in range(N_KV):
            lanes = slice(g * HEAD_DIM, (g + 1) * HEAD_DIM)

            def kv_rows(prev_ref, cur_ref, next_ref):
                lo = prev_ref[:, lanes] if j == 0 else cur_ref[(j - 1) * BLOCK:j * BLOCK, lanes]
                mid = cur_ref[j * BLOCK:(j + 1) * BLOCK, lanes]
                hi = (next_ref[:, lanes] if j == nsub - 1
                      else cur_ref[(j + 1) * BLOCK:(j + 2) * BLOCK, lanes])
                return jnp.concatenate([lo, mid, hi], axis=0)

            kk = kv_rows(kp_ref, kc_ref, kn_ref)
            vv = kv_rows(vp_ref, vc_ref, vn_ref)
            qs = jnp.concatenate(
                [q_ref[j * BLOCK:(j + 1) * BLOCK,
                       (g * GROUP + r) * HEAD_DIM:(g * GROUP + r + 1) * HEAD_DIM]
                 for r in range(GROUP)], axis=0)
            s = lax.dot_general(qs, kk, (((1,), (1,)), ((), ())),
                                preferred_element_type=_F32)
            s = s * scale + bias4
            sink = jnp.concatenate(
                [jnp.full((BLOCK, 1), sink_ref[g * GROUP + r], _F32) for r in range(GROUP)],
                axis=0)
            mx = jnp.maximum(jnp.max(s, axis=-1, keepdims=True), sink)
            p = jnp.exp(s - mx)
            denom = jnp.sum(p, axis=-1, keepdims=True) + jnp.exp(sink - mx)
            o = jnp.dot(p.astype(_BF16), vv, preferred_element_type=_F32)
            o = o * (1.0 / denom)
            for r in range(GROUP):
                cols = slice((g * GROUP + r) * HEAD_DIM, (g * GROUP + r + 1) * HEAD_DIM)
                gate = ga_ref[j * BLOCK:(j + 1) * BLOCK, cols].astype(_F32)
                o_ref[j * BLOCK:(j + 1) * BLOCK, cols] = (
                    o[r * BLOCK:(r + 1) * BLOCK, :] * gate).astype(_BF16)


def _attn(q, k, v, ga, sink, batch, seq):
    tq = TQ
    assert seq % tq == 0
    sub = tq // BLOCK
    nblk = seq // BLOCK
    q3 = q.reshape(batch, seq, ATTN_WIDTH)
    k3 = k.reshape(batch, seq, KV_WIDTH)
    v3 = v.reshape(batch, seq, KV_WIDTH)
    ga3 = ga.reshape(batch, seq, ATTN_WIDTH)
    cur = lambda b, n: (b, n, 0)
    prev = lambda b, n: (b, jnp.maximum(n * sub - 1, 0), 0)
    nxt = lambda b, n: (b, jnp.minimum(n * sub + sub, nblk - 1), 0)
    kv_specs = [
        pl.BlockSpec((None, BLOCK, KV_WIDTH), prev),
        pl.BlockSpec((None, tq, KV_WIDTH), cur),
        pl.BlockSpec((None, BLOCK, KV_WIDTH), nxt),
    ]
    out = pl.pallas_call(
        functools.partial(_attn_kernel, seq),
        grid=(batch, seq // tq),
        in_specs=[pl.BlockSpec(memory_space=pltpu.SMEM),
                  pl.BlockSpec((None, tq, ATTN_WIDTH), cur)]
                 + kv_specs + kv_specs
                 + [pl.BlockSpec((None, tq, ATTN_WIDTH), cur)],
        out_specs=pl.BlockSpec((None, tq, ATTN_WIDTH), cur),
        out_shape=jax.ShapeDtypeStruct((batch, seq, ATTN_WIDTH), _BF16),
        compiler_params=pltpu.CompilerParams(
            dimension_semantics=("arbitrary", "arbitrary"),
            vmem_limit_bytes=VMEM_LIMIT_BYTES),
        name="window_attn",
    )(sink, q3, k3, k3, k3, v3, v3, v3, ga3)
    return out.reshape(batch * seq, ATTN_WIDTH)


def _out_proj_kernel(a_ref, u_ref, vg_ref, gg_ref, x_ref, p_ref,
                     lng_ref, lnb_ref, ws_ref, bsb_ref, wout_ref, postg_ref,
                     wpe_ref, wpg_ref, o_ref, am_ref, vn_ref):
    tm = x_ref.shape[0]
    nchunks = tm // CHUNK

    vf = vg_ref[...].astype(_F32)
    mu = jnp.mean(vf, axis=-1, keepdims=True)
    vc = vf - mu
    var = jnp.mean(vc * vc, axis=-1, keepdims=True)
    vn = vc * lax.rsqrt(var + EPS) * lng_ref[...] + lnb_ref[...]
    vn_ref[...] = vn.astype(_BF16)

    am_ref[:, :ATTN_WIDTH] = a_ref[...]
    for h in range(N_GMLP_HEADS):
        cols = slice(h * HEAD_DIM, (h + 1) * HEAD_DIM)
        rhs = jnp.concatenate(
            [vn_ref[c * CHUNK:(c + 1) * CHUNK, cols] for c in range(nchunks)], axis=1)
        mixed = jnp.dot(ws_ref[h], rhs, preferred_element_type=_F32)
        for c in range(nchunks):
            rws = slice(c * CHUNK, (c + 1) * CHUNK)
            mx = mixed[:, c * HEAD_DIM:(c + 1) * HEAD_DIM] + bsb_ref[h]
            m = u_ref[rws, cols].astype(_F32) * mx * gg_ref[rws, cols].astype(_F32)
            am_ref[rws, ATTN_WIDTH + h * HEAD_DIM:ATTN_WIDTH + (h + 1) * HEAD_DIM] = m.astype(_BF16)

    y = jnp.dot(am_ref[...], wout_ref[...], preferred_element_type=_F32)
    ms = jnp.mean(y * y, axis=-1, keepdims=True)
    x1 = x_ref[...] + y * lax.rsqrt(ms + EPS) * postg_ref[...]
    gate = _sigmoid(jnp.dot(x1.astype(_BF16), wpg_ref[...], preferred_element_type=_F32))
    pe = jnp.dot(p_ref[...].astype(_BF16), wpe_ref[...], preferred_element_type=_F32)
    o_ref[...] = x1 + gate * pe


def _out_proj(a, u, vg, gg, x2d, p2d, ln_g, ln_b, ws_bf, bs_b, w_out_bf, post_g,
              w_pe_bf, w_pg_bf):
    m = x2d.shape[0]
    tm = TM_OUT
    assert m % tm == 0
    row = lambda i: (i, 0)
    const2 = lambda i: (0, 0)
    const3 = lambda i: (0, 0, 0)
    resident = dict(pipeline_mode=pl.Buffered(1))
    return pl.pallas_call(
        _out_proj_kernel,
        grid=(m // tm,),
        in_specs=[
            pl.BlockSpec((tm, ATTN_WIDTH), row),
            pl.BlockSpec((tm, GMLP_WIDTH), row),
            pl.BlockSpec((tm, GMLP_WIDTH), row),
            pl.BlockSpec((tm, GMLP_WIDTH), row),
            pl.BlockSpec((tm, D_MODEL), row),
            pl.BlockSpec((tm, PLE_DIM), row),
            pl.BlockSpec((1, GMLP_WIDTH), const2),
            pl.BlockSpec((1, GMLP_WIDTH), const2),
            pl.BlockSpec((N_GMLP_HEADS, CHUNK, CHUNK), const3, **resident),
            pl.BlockSpec((N_GMLP_HEADS, CHUNK, HEAD_DIM), const3, **resident),
            pl.BlockSpec((D_MODEL, D_MODEL), const2, **resident),
            pl.BlockSpec((1, D_MODEL), const2),
            pl.BlockSpec((PLE_DIM, D_MODEL), const2, **resident),
            pl.BlockSpec((D_MODEL, D_MODEL), const2, **resident),
        ],
        out_specs=pl.BlockSpec((tm, D_MODEL), row),
        out_shape=jax.ShapeDtypeStruct((m, D_MODEL), _F32),
        scratch_shapes=[pltpu.VMEM((tm, D_MODEL), _BF16),
                        pltpu.VMEM((tm, GMLP_WIDTH), _BF16)],
        compiler_params=pltpu.CompilerParams(
            dimension_semantics=("arbitrary",),
            vmem_limit_bytes=VMEM_LIMIT_BYTES),
        name="out_proj",
    )(a, u, vg, gg, x2d, p2d, ln_g, ln_b, ws_bf, bs_b, w_out_bf, post_g, w_pe_bf, w_pg_bf)


def _rope_tables(seq):
    inv = 1.0 / (ROPE_THETA ** (jnp.arange(0, HEAD_DIM, 2, dtype=_F32) / HEAD_DIM))
    ang = jnp.arange(seq, dtype=_F32)[:, None] * inv[None, :]
    cos = jnp.cos(ang)
    sin = jnp.sin(ang)
    return (jnp.concatenate([cos, cos], axis=-1),
            jnp.concatenate([-sin, sin], axis=-1))


def _layer(x, p, params, tables):
    (pre_g, w_in_bf, sink, ln_g, ln_b, ws_bf, bs_b, w_out_bf, post_g, w_pe_bf, w_pg_bf) = params
    batch, seq, _ = x.shape
    cos_t, sin_t = tables
    x2d = x.reshape(batch * seq, D_MODEL)
    p2d = p.reshape(batch * seq, PLE_DIM)
    q, k, v, ga, u, vg, gg = _in_proj(x2d, seq, pre_g, w_in_bf, cos_t[:seq], sin_t[:seq])
    a = _attn(q, k, v, ga, sink, batch, seq)
    out = _out_proj(a, u, vg, gg, x2d, p2d, ln_g, ln_b, ws_bf, bs_b, w_out_bf, post_g,
                    w_pe_bf, w_pg_bf)
    return out.reshape(batch, seq, D_MODEL)


def kernel(x_prompt, x_sample, p_prompt, p_sample, pre_norm_g, w_in, attn_sink,
           gmlp_ln_g, gmlp_ln_b, gmlp_ws, gmlp_bs, w_out, post_norm_g, w_pe, w_pg):
    depth = w_in.shape[0]
    tables = _rope_tables(max(x_prompt.shape[1], x_sample.shape[1]))
    y_prompt, y_sample = x_prompt, x_sample
    for i in range(depth):
        params = (
            pre_norm_g[i].reshape(1, D_MODEL),
            w_in[i].astype(_BF16),
            attn_sink[i],
            gmlp_ln_g[i].reshape(1, GMLP_WIDTH),
            gmlp_ln_b[i].reshape(1, GMLP_WIDTH),
            gmlp_ws[i].astype(_BF16),
            jnp.broadcast_to(gmlp_bs[i][:, :, None], (N_GMLP_HEADS, CHUNK, HEAD_DIM)),
            w_out[i].astype(_BF16),
            post_norm_g[i].reshape(1, D_MODEL),
            w_pe[i].astype(_BF16),
            w_pg[i].astype(_BF16),
        )
        y_prompt = _layer(y_prompt, p_prompt[i], params, tables)
        y_sample = _layer(y_sample, p_sample[i], params, tables)
    return (y_prompt, y_sample)
```

```python
import functools
import math

import jax
import jax.numpy as jnp
from jax import lax
from jax.experimental import pallas as pl
from jax.experimental.pallas import tpu as pltpu

D_MODEL = 2048
HEAD_DIM = 128
ATTN_WIDTH = 1024
N_HEADS = 8
N_KV = 2
GROUP = 4
KV_WIDTH = 256
WINDOW = 128
BLOCK = 128
ROPE_THETA = 10000.0
GMLP_WIDTH = 1024
N_GMLP_HEADS = 8
CHUNK = 128
PLE_DIM = 256
EPS = 1e-6
IN_WIDTH = 5632
NEG_INF = -1e30

VMEM_LIMIT_BYTES = 56 * 1024 * 1024

TM_IN = 512
NCHUNK = 512
TQ = 512
TM_OUT = 512
SUB_OUT = 256

_BF16 = jnp.bfloat16
_F32 = jnp.float32


def _sigmoid(x):
    return 1.0 / (1.0 + jnp.exp(-x))


def _gelu_exact(x):
    return 0.5 * x * (1.0 + lax.erf(x * (2.0 ** -0.5)))


def _in_proj_kernel(x_ref, g_ref, w_ref, cos_ref, sin_ref, lng_ref, lnb_ref, ws_ref, bsb_ref,
                    q_ref, k_ref, v_ref, ga_ref, m_ref, h_ref, vgf_ref, vn_ref):
    tm = x_ref.shape[0]
    rows = 64
    for r in range(tm // rows):
        x = x_ref[r * rows:(r + 1) * rows, :]
        ms = jnp.mean(x * x, axis=-1, keepdims=True)
        h = x * lax.rsqrt(ms + EPS) * g_ref[...]
        h_ref[r * rows:(r + 1) * rows, :] = h.astype(_BF16)

    cos = cos_ref[...]
    sin = sin_ref[...]

    def rope(t):
        return t * cos + pltpu.roll(t, HEAD_DIM // 2, 1) * sin

    def proj(c):
        return jnp.dot(h_ref[...], w_ref[:, c * NCHUNK:(c + 1) * NCHUNK],
                       preferred_element_type=_F32)

    heads_per_chunk = NCHUNK // HEAD_DIM
    for c in range(2):
        vgf_ref[:, c * NCHUNK:(c + 1) * NCHUNK] = _gelu_exact(proj(7 + c))
    for r in range(tm // rows):
        rs = slice(r * rows, (r + 1) * rows)
        vf = vgf_ref[rs, :]
        mu = jnp.mean(vf, axis=-1, keepdims=True)
        vc = vf - mu
        var = jnp.mean(vc * vc, axis=-1, keepdims=True)
        vn = vc * lax.rsqrt(var + EPS) * lng_ref[...] + lnb_ref[...]
        vn_ref[rs, :] = vn.astype(_BF16)
    nchunks = tm // CHUNK
    for c in range(2):
        gu = _gelu_exact(proj(5 + c))
        acc = proj(9 + c)
        sg = acc * _sigmoid(acc)
        for hd in range(heads_per_chunk):
            h = c * heads_per_chunk + hd
            cols = slice(h * HEAD_DIM, (h + 1) * HEAD_DIM)
            lcols = slice(hd * HEAD_DIM, (hd + 1) * HEAD_DIM)
            rhs = jnp.concatenate(
                [vn_ref[rc * CHUNK:(rc + 1) * CHUNK, cols] for rc in range(nchunks)], axis=1)
            mixed = jnp.dot(ws_ref[h], rhs, preferred_element_type=_F32)
            for rc in range(nchunks):
                rws = slice(rc * CHUNK, (rc + 1) * CHUNK)
                mx = mixed[:, rc * HEAD_DIM:(rc + 1) * HEAD_DIM] + bsb_ref[h]
                m_ref[rws, cols] = (gu[rws, lcols] * mx * sg[rws, lcols]).astype(_BF16)
    for c in range(2):
        acc = proj(c)
        for hd in range(heads_per_chunk):
            col = c * NCHUNK + hd * HEAD_DIM
            q_ref[:, col:col + HEAD_DIM] = rope(
                acc[:, hd * HEAD_DIM:(hd + 1) * HEAD_DIM]).astype(_BF16)
    for c in range(2):
        acc = proj(3 + c)
        ga_ref[:, c * NCHUNK:(c + 1) * NCHUNK] = (acc * _sigmoid(acc)).astype(_BF16)
    acc = proj(2)
    for hd in range(N_KV):
        k_ref[:, hd * HEAD_DIM:(hd + 1) * HEAD_DIM] = rope(
            acc[:, hd * HEAD_DIM:(hd + 1) * HEAD_DIM]).astype(_BF16)
    v_ref[...] = acc[:, KV_WIDTH:].astype(_BF16)


def _in_proj(x2d, seq, pre_g, w_in_bf, cos_t, sin_t, ln_g, ln_b, ws_bf, bs_b):
    m = x2d.shape[0]
    tm = TM_IN
    assert m % tm == 0 and seq % tm == 0 and tm % CHUNK == 0
    blocks_per_seq = seq // tm
    const = lambda i: (0, 0)
    const3 = lambda i: (0, 0, 0)
    row = lambda i: (i, 0)
    resident = dict(pipeline_mode=pl.Buffered(1))
    out_widths = (ATTN_WIDTH, KV_WIDTH, KV_WIDTH, ATTN_WIDTH, GMLP_WIDTH)
    return pl.pallas_call(
        _in_proj_kernel,
        grid=(m // tm,),
        in_specs=[
            pl.BlockSpec((tm, D_MODEL), row),
            pl.BlockSpec((1, D_MODEL), const),
            pl.BlockSpec((D_MODEL, IN_WIDTH), const, **resident),
            pl.BlockSpec((tm, HEAD_DIM), lambda i: (i % blocks_per_seq, 0)),
            pl.BlockSpec((tm, HEAD_DIM), lambda i: (i % blocks_per_seq, 0)),
            pl.BlockSpec((1, GMLP_WIDTH), const),
            pl.BlockSpec((1, GMLP_WIDTH), const),
            pl.BlockSpec((N_GMLP_HEADS, CHUNK, CHUNK), const3, **resident),
            pl.BlockSpec((N_GMLP_HEADS, CHUNK, HEAD_DIM), const3, **resident),
        ],
        out_specs=[pl.BlockSpec((tm, w), row) for w in out_widths],
        out_shape=[jax.ShapeDtypeStruct((m, w), _BF16) for w in out_widths],
        scratch_shapes=[pltpu.VMEM((tm, D_MODEL), _BF16),
                        pltpu.VMEM((tm, GMLP_WIDTH), _F32),
                        pltpu.VMEM((tm, GMLP_WIDTH), _BF16)],
        compiler_params=pltpu.CompilerParams(
            dimension_semantics=("arbitrary",),
            vmem_limit_bytes=VMEM_LIMIT_BYTES),
        name="in_proj",
    )(x2d, pre_g, w_in_bf, cos_t, sin_t, ln_g, ln_b, ws_bf, bs_b)


def _attn_kernel(seq, sink_ref, q_ref, kp_ref, kc_ref, kn_ref,
                 vp_ref, vc_ref, vn_ref, ga_ref, o_ref):
    n = pl.program_id(1)
    tq = q_ref.shape[0]
    nsub = tq // BLOCK
    scale = HEAD_DIM ** -0.5

    iq = lax.broadcasted_iota(jnp.int32, (BLOCK, 3 * BLOCK), 0)
    ik = lax.broadcasted_iota(jnp.int32, (BLOCK, 3 * BLOCK), 1)
    band = jnp.abs(iq + BLOCK - ik) <= WINDOW
    ik_row = lax.broadcasted_iota(jnp.int32, (1, 3 * BLOCK), 1)

    for j in range(nsub):
        kbase = n * tq + (j - 1) * BLOCK
        in_seq = (ik_row + kbase >= 0) & (ik_row + kbase < seq)
        bias = jnp.where(band & in_seq, 0.0, NEG_INF).astype(_F32)
        bias4 = jnp.concatenate([bias] * GROUP, axis=0)
        for g in range(N_KV):
            lanes = slice(g * HEAD_DIM, (g + 1) * HEAD_DIM)

            def kv_rows(prev_ref, cur_ref, next_ref):
                lo = prev_ref[:, lanes] if j == 0 else cur_ref[(j - 1) * BLOCK:j * BLOCK, lanes]
                mid = cur_ref[j * BLOCK:(j + 1) * BLOCK, lanes]
                hi = (next_ref[:, lanes] if j == nsub - 1
                      else cur_ref[(j + 1) * BLOCK:(j + 2) * BLOCK, lanes])
                return jnp.concatenate([lo, mid, hi], axis=0)

            kk = kv_rows(kp_ref, kc_ref, kn_ref)
            vv = jnp.concatenate(
                [kv_rows(vp_ref, vc_ref, vn_ref),
                 jnp.ones((3 * BLOCK, HEAD_DIM), _BF16)], axis=1)
            qs = jnp.concatenate(
                [q_ref[j * BLOCK:(j + 1) * BLOCK,
                       (g * GROUP + r) * HEAD_DIM:(g * GROUP + r + 1) * HEAD_DIM]
                 for r in range(GROUP)], axis=0)
            s = lax.dot_general(qs, kk, (((1,), (1,)), ((), ())),
                                preferred_element_type=_F32)
            s = s * scale + bias4
            sink = jnp.concatenate(
                [jnp.full((BLOCK, HEAD_DIM), sink_ref[g * GROUP + r], _F32)
                 for r in range(GROUP)], axis=0)
            sb = [s[:, i * BLOCK:(i + 1) * BLOCK] for i in range(3)]
            rowmax = jnp.max(jnp.maximum(jnp.maximum(sb[0], sb[1]), sb[2]),
                             axis=-1, keepdims=True)
            mx = jnp.maximum(jnp.broadcast_to(rowmax, (GROUP * BLOCK, HEAD_DIM)), sink)
            p = jnp.concatenate([jnp.exp(t - mx) for t in sb], axis=1).astype(_BF16)
            o = jnp.dot(p, vv, preferred_element_type=_F32)
            denom = o[:, HEAD_DIM:] + jnp.exp(sink - mx)
            o = o[:, :HEAD_DIM] * (1.0 / denom)
            for r in range(GROUP):
                cols = slice((g * GROUP + r) * HEAD_DIM, (g * GROUP + r + 1) * HEAD_DIM)
                gate = ga_ref[j * BLOCK:(j + 1) * BLOCK, cols].astype(_F32)
                o_ref[j * BLOCK:(j + 1) * BLOCK, cols] = (
                    o[r * BLOCK:(r + 1) * BLOCK, :] * gate).astype(_BF16)


def _attn(q, k, v, ga, sink, batch, seq):
    tq = TQ
    assert seq % tq == 0
    sub = tq // BLOCK
    nblk = seq // BLOCK
    q3 = q.reshape(batch, seq, ATTN_WIDTH)
    k3 = k.reshape(batch, seq, KV_WIDTH)
    v3 = v.reshape(batch, seq, KV_WIDTH)
    ga3 = ga.reshape(batch, seq, ATTN_WIDTH)
    cur = lambda b, n: (b, n, 0)
    prev = lambda b, n: (b, jnp.maximum(n * sub - 1, 0), 0)
    nxt = lambda b, n: (b, jnp.minimum(n * sub + sub, nblk - 1), 0)
    kv_specs = [
        pl.BlockSpec((None, BLOCK, KV_WIDTH), prev),
        pl.BlockSpec((None, tq, KV_WIDTH), cur),
        pl.BlockSpec((None, BLOCK, KV_WIDTH), nxt),
    ]
    out = pl.pallas_call(
        functools.partial(_attn_kernel, seq),
        grid=(batch, seq // tq),
        in_specs=[pl.BlockSpec(memory_space=pltpu.SMEM),
                  pl.BlockSpec((None, tq, ATTN_WIDTH), cur)]
                 + kv_specs + kv_specs
                 + [pl.BlockSpec((None, tq, ATTN_WIDTH), cur)],
        out_specs=pl.BlockSpec((None, tq, ATTN_WIDTH), cur),
        out_shape=jax.ShapeDtypeStruct((batch, seq, ATTN_WIDTH), _BF16),
        compiler_params=pltpu.CompilerParams(
            dimension_semantics=("arbitrary", "arbitrary"),
            vmem_limit_bytes=VMEM_LIMIT_BYTES),
        name="window_attn",
    )(sink, q3, k3, k3, k3, v3, v3, v3, ga3)
    return out.reshape(batch * seq, ATTN_WIDTH)


def _out_proj_kernel(a_ref, m_ref, x_ref, p_ref, wout_ref, postg_ref, wpe_ref, wpg_ref, o_ref):
    tm = x_ref.shape[0]
    subs = [slice(t * SUB_OUT, (t + 1) * SUB_OUT) for t in range(tm // SUB_OUT)]

    def mix_proj(sub):
        return (jnp.dot(a_ref[sub, :], wout_ref[:ATTN_WIDTH, :], preferred_element_type=_F32)
                + jnp.dot(m_ref[sub, :], wout_ref[ATTN_WIDTH:, :], preferred_element_type=_F32))

    def post_norm(sub, y):
        ms = jnp.mean(y * y, axis=-1, keepdims=True)
        return x_ref[sub, :] + y * lax.rsqrt(ms + EPS) * postg_ref[...]

    def ple(sub, x1):
        gate = _sigmoid(jnp.dot(x1.astype(_BF16), wpg_ref[...], preferred_element_type=_F32))
        pe = jnp.dot(p_ref[sub, :].astype(_BF16), wpe_ref[...], preferred_element_type=_F32)
        o_ref[sub, :] = x1 + gate * pe

    ys = [mix_proj(sub) for sub in subs]
    for sub, y in zip(subs, ys):
        ple(sub, post_norm(sub, y))


def _out_proj(a, m_gated, x2d, p2d, w_out_bf, post_g, w_pe_bf, w_pg_bf):
    m = x2d.shape[0]
    tm = TM_OUT
    assert m % tm == 0 and tm % SUB_OUT == 0
    row = lambda i: (i, 0)
    const2 = lambda i: (0, 0)
    resident = dict(pipeline_mode=pl.Buffered(1))
    return pl.pallas_call(
        _out_proj_kernel,
        grid=(m // tm,),
        in_specs=[
            pl.BlockSpec((tm, ATTN_WIDTH), row),
            pl.BlockSpec((tm, GMLP_WIDTH), row),
            pl.BlockSpec((tm, D_MODEL), row),
            pl.BlockSpec((tm, PLE_DIM), row),
            pl.BlockSpec((D_MODEL, D_MODEL), const2, **resident),
            pl.BlockSpec((1, D_MODEL), const2),
            pl.BlockSpec((PLE_DIM, D_MODEL), const2, **resident),
            pl.BlockSpec((D_MODEL, D_MODEL), const2, **resident),
        ],
        out_specs=pl.BlockSpec((tm, D_MODEL), row),
        out_shape=jax.ShapeDtypeStruct((m, D_MODEL), _F32),
        compiler_params=pltpu.CompilerParams(
            dimension_semantics=("arbitrary",),
            vmem_limit_bytes=VMEM_LIMIT_BYTES),
        name="out_proj",
    )(a, m_gated, x2d, p2d, w_out_bf, post_g, w_pe_bf, w_pg_bf)


def _rope_tables(seq):
    inv = 1.0 / (ROPE_THETA ** (jnp.arange(0, HEAD_DIM, 2, dtype=_F32) / HEAD_DIM))
    ang = jnp.arange(seq, dtype=_F32)[:, None] * inv[None, :]
    cos = jnp.cos(ang)
    sin = jnp.sin(ang)
    return (jnp.concatenate([cos, cos], axis=-1),
            jnp.concatenate([-sin, sin], axis=-1))


def _layer(x, p, params, tables):
    (pre_g, w_in_bf, sink, ln_g, ln_b, ws_bf, bs_b, w_out_bf, post_g, w_pe_bf, w_pg_bf) = params
    batch, seq, _ = x.shape
    cos_t, sin_t = tables
    x2d = x.reshape(batch * seq, D_MODEL)
    p2d = p.reshape(batch * seq, PLE_DIM)
    q, k, v, ga, m_gated = _in_proj(x2d, seq, pre_g, w_in_bf, cos_t[:seq], sin_t[:seq],
                                    ln_g, ln_b, ws_bf, bs_b)
    a = _attn(q, k, v, ga, sink, batch, seq)
    out = _out_proj(a, m_gated, x2d, p2d, w_out_bf, post_g, w_pe_bf, w_pg_bf)
    return out.reshape(batch, seq, D_MODEL)


def kernel(x_prompt, x_sample, p_prompt, p_sample, pre_norm_g, w_in, attn_sink,
           gmlp_ln_g, gmlp_ln_b, gmlp_ws, gmlp_bs, w_out, post_norm_g, w_pe, w_pg):
    depth = w_in.shape[0]
    tables = _rope_tables(max(x_prompt.shape[1], x_sample.shape[1]))
    y_prompt, y_sample = x_prompt, x_sample
    for i in range(depth):
        params = (
            pre_norm_g[i].reshape(1, D_MODEL),
            w_in[i].astype(_BF16),
            attn_sink[i],
            gmlp_ln_g[i].reshape(1, GMLP_WIDTH),
            gmlp_ln_b[i].reshape(1, GMLP_WIDTH),
            gmlp_ws[i].astype(_BF16),
            jnp.broadcast_to(gmlp_bs[i][:, :, None], (N_GMLP_HEADS, CHUNK, HEAD_DIM)),
            w_out[i].astype(_BF16),
            post_norm_g[i].reshape(1, D_MODEL),
            w_pe[i].astype(_BF16),
            w_pg[i].astype(_BF16),
        )
        y_prompt = _layer(y_prompt, p_prompt[i], params, tables)
        y_sample = _layer(y_sample, p_sample[i], params, tables)
    return (y_prompt, y_sample)
```

```python
import functools
import math

import jax
import jax.numpy as jnp
from jax import lax
from jax.experimental import pallas as pl
from jax.experimental.pallas import tpu as pltpu

D_MODEL = 2048
HEAD_DIM = 128
ATTN_WIDTH = 1024
N_HEADS = 8
N_KV = 2
GROUP = 4
KV_WIDTH = 256
WINDOW = 128
BLOCK = 128
ROPE_THETA = 10000.0
GMLP_WIDTH = 1024
N_GMLP_HEADS = 8
CHUNK = 128
PLE_DIM = 256
EPS = 1e-6
IN_WIDTH = 5632
NEG_INF = -1e30
LOG2E = math.log2(math.e)
Q_SCALE = HEAD_DIM ** -0.5 * LOG2E

VMEM_LIMIT_BYTES = 56 * 1024 * 1024

TM_IN = 512
NCHUNK = 512
TQ = 512
TM_OUT = 512
SUB_OUT = 256

_BF16 = jnp.bfloat16
_F32 = jnp.float32


def _sigmoid(x):
    return 1.0 / (1.0 + jnp.exp(-x))


def _gelu_exact(x):
    return 0.5 * x * (1.0 + lax.erf(x * (2.0 ** -0.5)))


def _in_proj_kernel(x_ref, g_ref, w_ref, cos_ref, sin_ref, lng_ref, lnb_ref, ws_ref, bsb_ref,
                    q_ref, k_ref, v_ref, ga_ref, m_ref, h_ref, vgf_ref, vn_ref):
    tm = x_ref.shape[0]
    rows = 64
    for r in range(tm // rows):
        x = x_ref[r * rows:(r + 1) * rows, :]
        ms = jnp.mean(x * x, axis=-1, keepdims=True)
        h = x * lax.rsqrt(ms + EPS) * g_ref[...]
        h_ref[r * rows:(r + 1) * rows, :] = h.astype(_BF16)

    cos = cos_ref[...]
    sin = sin_ref[...]

    def rope(t):
        return t * cos + pltpu.roll(t, HEAD_DIM // 2, 1) * sin

    def proj(c):
        return jnp.dot(h_ref[...], w_ref[:, c * NCHUNK:(c + 1) * NCHUNK],
                       preferred_element_type=_F32)

    heads_per_chunk = NCHUNK // HEAD_DIM
    for c in range(2):
        vgf_ref[:, c * NCHUNK:(c + 1) * NCHUNK] = _gelu_exact(proj(7 + c))
    for r in range(tm // rows):
        rs = slice(r * rows, (r + 1) * rows)
        vf = vgf_ref[rs, :]
        mu = jnp.mean(vf, axis=-1, keepdims=True)
        vc = vf - mu
        var = jnp.mean(vc * vc, axis=-1, keepdims=True)
        vn = vc * lax.rsqrt(var + EPS) * lng_ref[...] + lnb_ref[...]
        vn_ref[rs, :] = vn.astype(_BF16)
    nchunks = tm // CHUNK
    for c in range(2):
        gu = _gelu_exact(proj(5 + c))
        acc = proj(9 + c)
        sg = acc * _sigmoid(acc)
        for hd in range(heads_per_chunk):
            h = c * heads_per_chunk + hd
            cols = slice(h * HEAD_DIM, (h + 1) * HEAD_DIM)
            lcols = slice(hd * HEAD_DIM, (hd + 1) * HEAD_DIM)
            rhs = jnp.concatenate(
                [vn_ref[rc * CHUNK:(rc + 1) * CHUNK, cols] for rc in range(nchunks)], axis=1)
            mixed = jnp.dot(ws_ref[h], rhs, preferred_element_type=_F32)
            for rc in range(nchunks):
                rws = slice(rc * CHUNK, (rc + 1) * CHUNK)
                mx = mixed[:, rc * HEAD_DIM:(rc + 1) * HEAD_DIM] + bsb_ref[h]
                m_ref[rws, cols] = (gu[rws, lcols] * mx * sg[rws, lcols]).astype(_BF16)
    for c in range(2):
        acc = proj(c)
        for hd in range(heads_per_chunk):
            col = c * NCHUNK + hd * HEAD_DIM
            q_ref[:, col:col + HEAD_DIM] = (rope(
                acc[:, hd * HEAD_DIM:(hd + 1) * HEAD_DIM]) * Q_SCALE).astype(_BF16)
    for c in range(2):
        acc = proj(3 + c)
        ga_ref[:, c * NCHUNK:(c + 1) * NCHUNK] = (acc * _sigmoid(acc)).astype(_BF16)
    acc = proj(2)
    for hd in range(N_KV):
        k_ref[:, hd * HEAD_DIM:(hd + 1) * HEAD_DIM] = rope(
            acc[:, hd * HEAD_DIM:(hd + 1) * HEAD_DIM]).astype(_BF16)
    v_ref[...] = acc[:, KV_WIDTH:].astype(_BF16)


def _in_proj(x2d, seq, pre_g, w_in_bf, cos_t, sin_t, ln_g, ln_b, ws_bf, bs_b):
    m = x2d.shape[0]
    tm = TM_IN
    assert m % tm == 0 and seq % tm == 0 and tm % CHUNK == 0
    blocks_per_seq = seq // tm
    const = lambda i: (0, 0)
    const3 = lambda i: (0, 0, 0)
    row = lambda i: (i, 0)
    resident = dict(pipeline_mode=pl.Buffered(1))
    out_widths = (ATTN_WIDTH, KV_WIDTH, KV_WIDTH, ATTN_WIDTH, GMLP_WIDTH)
    return pl.pallas_call(
        _in_proj_kernel,
        grid=(m // tm,),
        in_specs=[
            pl.BlockSpec((tm, D_MODEL), row),
            pl.BlockSpec((1, D_MODEL), const),
            pl.BlockSpec((D_MODEL, IN_WIDTH), const, **resident),
            pl.BlockSpec((tm, HEAD_DIM), lambda i: (i % blocks_per_seq, 0)),
            pl.BlockSpec((tm, HEAD_DIM), lambda i: (i % blocks_per_seq, 0)),
            pl.BlockSpec((1, GMLP_WIDTH), const),
            pl.BlockSpec((1, GMLP_WIDTH), const),
            pl.BlockSpec((N_GMLP_HEADS, CHUNK, CHUNK), const3, **resident),
            pl.BlockSpec((N_GMLP_HEADS, CHUNK, HEAD_DIM), const3, **resident),
        ],
        out_specs=[pl.BlockSpec((tm, w), row) for w in out_widths],
        out_shape=[jax.ShapeDtypeStruct((m, w), _BF16) for w in out_widths],
        scratch_shapes=[pltpu.VMEM((tm, D_MODEL), _BF16),
                        pltpu.VMEM((tm, GMLP_WIDTH), _F32),
                        pltpu.VMEM((tm, GMLP_WIDTH), _BF16)],
        compiler_params=pltpu.CompilerParams(
            dimension_semantics=("arbitrary",),
            vmem_limit_bytes=VMEM_LIMIT_BYTES),
        name="in_proj",
    )(x2d, pre_g, w_in_bf, cos_t, sin_t, ln_g, ln_b, ws_bf, bs_b)


def _attn_kernel(seq, sink_ref, q_ref, kp_ref, kc_ref, kn_ref,
                 vp_ref, vc_ref, vn_ref, ga_ref, o_ref):
    n = pl.program_id(1)
    tq = q_ref.shape[0]
    nsub = tq // BLOCK

    ik = lax.broadcasted_iota(jnp.int32, (3 * BLOCK, BLOCK), 0)
    iq = lax.broadcasted_iota(jnp.int32, (3 * BLOCK, BLOCK), 1)
    band_t = jnp.abs(iq + BLOCK - ik) <= WINDOW
    eye = (lax.broadcasted_iota(jnp.int32, (BLOCK, BLOCK), 0)
           == lax.broadcasted_iota(jnp.int32, (BLOCK, BLOCK), 1)).astype(_BF16)
    onehot = jnp.concatenate([eye] * GROUP, axis=0)

    for j in range(nsub):
        kbase = n * tq + (j - 1) * BLOCK
        in_seq = (ik + kbase >= 0) & (ik + kbase < seq)
        bias_t = jnp.where(band_t & in_seq, 0.0, NEG_INF).astype(_BF16)
        for g in range(N_KV):
            lanes = slice(g * HEAD_DIM, (g + 1) * HEAD_DIM)

            def kv_rows(prev_ref, cur_ref, next_ref):
                lo = prev_ref[:, lanes] if j == 0 else cur_ref[(j - 1) * BLOCK:j * BLOCK, lanes]
                mid = cur_ref[j * BLOCK:(j + 1) * BLOCK, lanes]
                hi = (next_ref[:, lanes] if j == nsub - 1
                      else cur_ref[(j + 1) * BLOCK:(j + 2) * BLOCK, lanes])
                return jnp.concatenate([lo, mid, hi], axis=0)

            kk = jnp.concatenate([kv_rows(kp_ref, kc_ref, kn_ref), bias_t], axis=1)
            vv = jnp.concatenate(
                [kv_rows(vp_ref, vc_ref, vn_ref),
                 jnp.ones((3 * BLOCK, HEAD_DIM), _BF16)], axis=1)
            qs = jnp.concatenate(
                [q_ref[j * BLOCK:(j + 1) * BLOCK,
                       (g * GROUP + r) * HEAD_DIM:(g * GROUP + r + 1) * HEAD_DIM]
                 for r in range(GROUP)], axis=0)
            s = lax.dot_general(jnp.concatenate([qs, onehot], axis=1), kk,
                                (((1,), (1,)), ((), ())),
                                preferred_element_type=_F32)
            sink = jnp.concatenate(
                [jnp.full((BLOCK, HEAD_DIM), sink_ref[g * GROUP + r] * LOG2E, _F32)
                 for r in range(GROUP)], axis=0)
            sb = [s[:, i * BLOCK:(i + 1) * BLOCK] for i in range(3)]
            rowmax = jnp.max(jnp.maximum(jnp.maximum(sb[0], sb[1]), sb[2]),
                             axis=-1, keepdims=True)
            mx = jnp.maximum(jnp.broadcast_to(rowmax, (GROUP * BLOCK, HEAD_DIM)), sink)
            p = jnp.concatenate([jnp.exp2(t - mx) for t in sb], axis=1).astype(_BF16)
            o = jnp.dot(p, vv, preferred_element_type=_F32)
            denom = o[:, HEAD_DIM:] + jnp.exp2(sink - mx)
            o = o[:, :HEAD_DIM] * (1.0 / denom)
            for r in range(GROUP):
                cols = slice((g * GROUP + r) * HEAD_DIM, (g * GROUP + r + 1) * HEAD_DIM)
                gate = ga_ref[j * BLOCK:(j + 1) * BLOCK, cols].astype(_F32)
                o_ref[j * BLOCK:(j + 1) * BLOCK, cols] = (
                    o[r * BLOCK:(r + 1) * BLOCK, :] * gate).astype(_BF16)


def _attn(q, k, v, ga, sink, batch, seq):
    tq = TQ
    assert seq % tq == 0
    sub = tq // BLOCK
    nblk = seq // BLOCK
    q3 = q.reshape(batch, seq, ATTN_WIDTH)
    k3 = k.reshape(batch, seq, KV_WIDTH)
    v3 = v.reshape(batch, seq, KV_WIDTH)
    ga3 = ga.reshape(batch, seq, ATTN_WIDTH)
    cur = lambda b, n: (b, n, 0)
    prev = lambda b, n: (b, jnp.maximum(n * sub - 1, 0), 0)
    nxt = lambda b, n: (b, jnp.minimum(n * sub + sub, nblk - 1), 0)
    kv_specs = [
        pl.BlockSpec((None, BLOCK, KV_WIDTH), prev),
        pl.BlockSpec((None, tq, KV_WIDTH), cur),
        pl.BlockSpec((None, BLOCK, KV_WIDTH), nxt),
    ]
    out = pl.pallas_call(
        functools.partial(_attn_kernel, seq),
        grid=(batch, seq // tq),
        in_specs=[pl.BlockSpec(memory_space=pltpu.SMEM),
                  pl.BlockSpec((None, tq, ATTN_WIDTH), cur)]
                 + kv_specs + kv_specs
                 + [pl.BlockSpec((None, tq, ATTN_WIDTH), cur)],
        out_specs=pl.BlockSpec((None, tq, ATTN_WIDTH), cur),
        out_shape=jax.ShapeDtypeStruct((batch, seq, ATTN_WIDTH), _BF16),
        compiler_params=pltpu.CompilerParams(
            dimension_semantics=("arbitrary", "arbitrary"),
            vmem_limit_bytes=VMEM_LIMIT_BYTES),
        name="window_attn",
    )(sink, q3, k3, k3, k3, v3, v3, v3, ga3)
    return out.reshape(batch * seq, ATTN_WIDTH)


def _out_proj_kernel(a_ref, m_ref, x_ref, p_ref, wout_ref, postg_ref, wpe_ref, wpg_ref, o_ref):
    tm = x_ref.shape[0]
    subs = [slice(t * SUB_OUT, (t + 1) * SUB_OUT) for t in range(tm // SUB_OUT)]

    def mix_proj(sub):
        return (jnp.dot(a_ref[sub, :], wout_ref[:ATTN_WIDTH, :], preferred_element_type=_F32)
                + jnp.dot(m_ref[sub, :], wout_ref[ATTN_WIDTH:, :], preferred_element_type=_F32))

    def post_norm(sub, y):
        ms = jnp.mean(y * y, axis=-1, keepdims=True)
        return x_ref[sub, :] + y * lax.rsqrt(ms + EPS) * postg_ref[...]

    def ple(sub, x1):
        gate = _sigmoid(jnp.dot(x1.astype(_BF16), wpg_ref[...], preferred_element_type=_F32))
        pe = jnp.dot(p_ref[sub, :].astype(_BF16), wpe_ref[...], preferred_element_type=_F32)
        o_ref[sub, :] = x1 + gate * pe

    ys = [mix_proj(sub) for sub in subs]
    for sub, y in zip(subs, ys):
        ple(sub, post_norm(sub, y))


def _out_proj(a, m_gated, x2d, p2d, w_out_bf, post_g, w_pe_bf, w_pg_bf):
    m = x2d.shape[0]
    tm = TM_OUT
    assert m % tm == 0 and tm % SUB_OUT == 0
    row = lambda i: (i, 0)
    const2 = lambda i: (0, 0)
    resident = dict(pipeline_mode=pl.Buffered(1))
    return pl.pallas_call(
        _out_proj_kernel,
        grid=(m // tm,),
        in_specs=[
            pl.BlockSpec((tm, ATTN_WIDTH), row),
            pl.BlockSpec((tm, GMLP_WIDTH), row),
            pl.BlockSpec((tm, D_MODEL), row),
            pl.BlockSpec((tm, PLE_DIM), row),
            pl.BlockSpec((D_MODEL, D_MODEL), const2, **resident),
            pl.BlockSpec((1, D_MODEL), const2),
            pl.BlockSpec((PLE_DIM, D_MODEL), const2, **resident),
            pl.BlockSpec((D_MODEL, D_MODEL), const2, **resident),
        ],
        out_specs=pl.BlockSpec((tm, D_MODEL), row),
        out_shape=jax.ShapeDtypeStruct((m, D_MODEL), _F32),
        compiler_params=pltpu.CompilerParams(
            dimension_semantics=("arbitrary",),
            vmem_limit_bytes=VMEM_LIMIT_BYTES),
        name="out_proj",
    )(a, m_gated, x2d, p2d, w_out_bf, post_g, w_pe_bf, w_pg_bf)


def _rope_tables(seq):
    inv = 1.0 / (ROPE_THETA ** (jnp.arange(0, HEAD_DIM, 2, dtype=_F32) / HEAD_DIM))
    ang = jnp.arange(seq, dtype=_F32)[:, None] * inv[None, :]
    cos = jnp.cos(ang)
    sin = jnp.sin(ang)
    return (jnp.concatenate([cos, cos], axis=-1),
            jnp.concatenate([-sin, sin], axis=-1))


def _layer(x, p, params, tables):
    (pre_g, w_in_bf, sink, ln_g, ln_b, ws_bf, bs_b, w_out_bf, post_g, w_pe_bf, w_pg_bf) = params
    batch, seq, _ = x.shape
    cos_t, sin_t = tables
    x2d = x.reshape(batch * seq, D_MODEL)
    p2d = p.reshape(batch * seq, PLE_DIM)
    q, k, v, ga, m_gated = _in_proj(x2d, seq, pre_g, w_in_bf, cos_t[:seq], sin_t[:seq],
                                    ln_g, ln_b, ws_bf, bs_b)
    a = _attn(q, k, v, ga, sink, batch, seq)
    out = _out_proj(a, m_gated, x2d, p2d, w_out_bf, post_g, w_pe_bf, w_pg_bf)
    return out.reshape(batch, seq, D_MODEL)


def kernel(x_prompt, x_sample, p_prompt, p_sample, pre_norm_g, w_in, attn_sink,
           gmlp_ln_g, gmlp_ln_b, gmlp_ws, gmlp_bs, w_out, post_norm_g, w_pe, w_pg):
    depth = w_in.shape[0]
    tables = _rope_tables(max(x_prompt.shape[1], x_sample.shape[1]))
    y_prompt, y_sample = x_prompt, x_sample
    for i in range(depth):
        params = (
            pre_norm_g[i].reshape(1, D_MODEL),
            w_in[i].astype(_BF16),
            attn_sink[i],
            gmlp_ln_g[i].reshape(1, GMLP_WIDTH),
            gmlp_ln_b[i].reshape(1, GMLP_WIDTH),
            gmlp_ws[i].astype(_BF16),
            jnp.broadcast_to(gmlp_bs[i][:, :, None], (N_GMLP_HEADS, CHUNK, HEAD_DIM)),
            w_out[i].astype(_BF16),
            post_norm_g[i].reshape(1, D_MODEL),
            w_pe[i].astype(_BF16),
            w_pg[i].astype(_BF16),
        )
        y_prompt = _layer(y_prompt, p_prompt[i], params, tables)
        y_sample = _layer(y_sample, p_sample[i], params, tables)
    return (y_prompt, y_sample)
```

```python
import functools
import math

import jax
import jax.numpy as jnp
from jax import lax
from jax.experimental import pallas as pl
from jax.experimental.pallas import tpu as pltpu

D_MODEL = 2048
HEAD_DIM = 128
ATTN_WIDTH = 1024
N_HEADS = 8
N_KV = 2
GROUP = 4
KV_WIDTH = 256
WINDOW = 128
BLOCK = 128
ROPE_THETA = 10000.0
GMLP_WIDTH = 1024
N_GMLP_HEADS = 8
CHUNK = 128
PLE_DIM = 256
EPS = 1e-6
IN_WIDTH = 5632
NEG_INF = -1e30
LOG2E = math.log2(math.e)
Q_SCALE = HEAD_DIM ** -0.5 * LOG2E

VMEM_LIMIT_BYTES = 56 * 1024 * 1024

TM_IN = 512
SUB_IN = 256
NCHUNK = 1024
TQ = 2048
TM_OUT = 512
SUB_OUT = 256

_BF16 = jnp.bfloat16
_F32 = jnp.float32


def _sigmoid(x):
    return 1.0 / (1.0 + jnp.exp(-x))


def _gelu_exact(x):
    return 0.5 * x * (1.0 + lax.erf(x * (2.0 ** -0.5)))


def _in_proj_kernel(x_ref, g_ref, w_ref, cos_ref, sin_ref, lng_ref, lnb_ref, ws_ref, bsb_ref,
                    q_ref, k_ref, v_ref, ga_ref, m_ref, h_ref, vgf_ref, vn_ref):
    tm = x_ref.shape[0]
    rows = 64
    for r in range(tm // rows):
        x = x_ref[r * rows:(r + 1) * rows, :]
        ms = jnp.mean(x * x, axis=-1, keepdims=True)
        h = x * lax.rsqrt(ms + EPS) * g_ref[...]
        h_ref[r * rows:(r + 1) * rows, :] = h.astype(_BF16)

    cos = cos_ref[...]
    sin = sin_ref[...]

    def rope(t):
        return t * cos + pltpu.roll(t, HEAD_DIM // 2, 1) * sin

    def proj(start, width):
        return jnp.concatenate(
            [jnp.dot(h_ref[r * SUB_IN:(r + 1) * SUB_IN, :], w_ref[:, start:start + width],
                     preferred_element_type=_F32) for r in range(tm // SUB_IN)], axis=0)

    q0, k0, ga0 = 0, ATTN_WIDTH, ATTN_WIDTH + 2 * KV_WIDTH
    u0 = ga0 + ATTN_WIDTH
    vg0 = u0 + GMLP_WIDTH
    gg0 = vg0 + GMLP_WIDTH
    nsplit = ATTN_WIDTH // NCHUNK
    heads_per_chunk = NCHUNK // HEAD_DIM
    nchunks = tm // CHUNK
    mix_ref = vgf_ref

    def gmlp_v():
        for c in range(nsplit):
            ccols = slice(c * NCHUNK, (c + 1) * NCHUNK)
            vgf_ref[:, ccols] = _gelu_exact(proj(vg0 + c * NCHUNK, NCHUNK))

    def layer_norm():
        for r in range(tm // rows):
            rs = slice(r * rows, (r + 1) * rows)
            vf = vgf_ref[rs, :]
            mu = jnp.mean(vf, axis=-1, keepdims=True)
            vc = vf - mu
            var = jnp.mean(vc * vc, axis=-1, keepdims=True)
            vn = vc * lax.rsqrt(var + EPS) * lng_ref[...] + lnb_ref[...]
            vn_ref[rs, :] = vn.astype(_BF16)

    def spatial_mix():
        for h in range(N_GMLP_HEADS):
            cols = slice(h * HEAD_DIM, (h + 1) * HEAD_DIM)
            rhs = jnp.concatenate(
                [vn_ref[rc * CHUNK:(rc + 1) * CHUNK, cols] for rc in range(nchunks)], axis=1)
            mixed = jnp.dot(ws_ref[h], rhs, preferred_element_type=_F32)
            for rc in range(nchunks):
                mix_ref[rc * CHUNK:(rc + 1) * CHUNK, cols] = (
                    mixed[:, rc * HEAD_DIM:(rc + 1) * HEAD_DIM] + bsb_ref[h])

    def gmlp_gate():
        for c in range(nsplit):
            ccols = slice(c * NCHUNK, (c + 1) * NCHUNK)
            gu = _gelu_exact(proj(u0 + c * NCHUNK, NCHUNK))
            acc = proj(gg0 + c * NCHUNK, NCHUNK)
            m_ref[:, ccols] = (gu * mix_ref[:, ccols] * (acc * _sigmoid(acc))).astype(_BF16)

    def attn_q():
        for c in range(nsplit):
            acc = proj(q0 + c * NCHUNK, NCHUNK)
            for hd in range(heads_per_chunk):
                col = c * NCHUNK + hd * HEAD_DIM
                q_ref[:, col:col + HEAD_DIM] = (rope(
                    acc[:, hd * HEAD_DIM:(hd + 1) * HEAD_DIM]) * Q_SCALE).astype(_BF16)

    def attn_gate():
        for c in range(nsplit):
            acc = proj(ga0 + c * NCHUNK, NCHUNK)
            ga_ref[:, c * NCHUNK:(c + 1) * NCHUNK] = (acc * _sigmoid(acc)).astype(_BF16)

    def attn_kv():
        acc = proj(k0, KV_WIDTH)
        for hd in range(N_KV):
            k_ref[:, hd * HEAD_DIM:(hd + 1) * HEAD_DIM] = rope(
                acc[:, hd * HEAD_DIM:(hd + 1) * HEAD_DIM]).astype(_BF16)
        v_ref[...] = proj(k0 + KV_WIDTH, KV_WIDTH).astype(_BF16)

    gmlp_v()
    attn_q()
    layer_norm()
    spatial_mix()
    gmlp_gate()
    attn_gate()
    attn_kv()


def _in_proj(x2d, seq, pre_g, w_in_bf, cos_t, sin_t, ln_g, ln_b, ws_bf, bs_b):
    m = x2d.shape[0]
    tm = TM_IN
    assert m % tm == 0 and seq % tm == 0 and tm % CHUNK == 0
    blocks_per_seq = seq // tm
    const = lambda i: (0, 0)
    const3 = lambda i: (0, 0, 0)
    row = lambda i: (i, 0)
    resident = dict(pipeline_mode=pl.Buffered(1))
    out_widths = (ATTN_WIDTH, KV_WIDTH, KV_WIDTH, ATTN_WIDTH, GMLP_WIDTH)
    return pl.pallas_call(
        _in_proj_kernel,
        grid=(m // tm,),
        in_specs=[
            pl.BlockSpec((tm, D_MODEL), row),
            pl.BlockSpec((1, D_MODEL), const),
            pl.BlockSpec((D_MODEL, IN_WIDTH), const, **resident),
            pl.BlockSpec((tm, HEAD_DIM), lambda i: (i % blocks_per_seq, 0)),
            pl.BlockSpec((tm, HEAD_DIM), lambda i: (i % blocks_per_seq, 0)),
            pl.BlockSpec((1, GMLP_WIDTH), const),
            pl.BlockSpec((1, GMLP_WIDTH), const),
            pl.BlockSpec((N_GMLP_HEADS, CHUNK, CHUNK), const3, **resident),
            pl.BlockSpec((N_GMLP_HEADS, CHUNK, HEAD_DIM), const3, **resident),
        ],
        out_specs=[pl.BlockSpec((tm, w), row) for w in out_widths],
        out_shape=[jax.ShapeDtypeStruct((m, w), _BF16) for w in out_widths],
        scratch_shapes=[pltpu.VMEM((tm, D_MODEL), _BF16),
                        pltpu.VMEM((tm, GMLP_WIDTH), _F32),
                        pltpu.VMEM((tm, GMLP_WIDTH), _BF16)],
        compiler_params=pltpu.CompilerParams(
            dimension_semantics=("arbitrary",),
            vmem_limit_bytes=VMEM_LIMIT_BYTES),
        name="in_proj",
    )(x2d, pre_g, w_in_bf, cos_t, sin_t, ln_g, ln_b, ws_bf, bs_b)


def _attn_kernel(seq, sink_ref, q_ref, kp_ref, kc_ref, kn_ref,
                 vp_ref, vc_ref, vn_ref, ga_ref, o_ref):
    n = pl.program_id(1)
    tq = q_ref.shape[0]
    nsub = tq // BLOCK

    ik = lax.broadcasted_iota(jnp.int32, (3 * BLOCK, BLOCK), 0)
    iq = lax.broadcasted_iota(jnp.int32, (3 * BLOCK, BLOCK), 1)
    band_t = jnp.abs(iq + BLOCK - ik) <= WINDOW
    eye = (lax.broadcasted_iota(jnp.int32, (BLOCK, BLOCK), 0)
           == lax.broadcasted_iota(jnp.int32, (BLOCK, BLOCK), 1)).astype(_BF16)
    onehot = jnp.concatenate([eye] * GROUP, axis=0)

    for j in range(nsub):
        kbase = n * tq + (j - 1) * BLOCK
        in_seq = (ik + kbase >= 0) & (ik + kbase < seq)
        bias_t = jnp.where(band_t & in_seq, 0.0, NEG_INF).astype(_BF16)
        for g in range(N_KV):
            lanes = slice(g * HEAD_DIM, (g + 1) * HEAD_DIM)

            def kv_rows(prev_ref, cur_ref, next_ref):
                lo = prev_ref[:, lanes] if j == 0 else cur_ref[(j - 1) * BLOCK:j * BLOCK, lanes]
                mid = cur_ref[j * BLOCK:(j + 1) * BLOCK, lanes]
                hi = (next_ref[:, lanes] if j == nsub - 1
                      else cur_ref[(j + 1) * BLOCK:(j + 2) * BLOCK, lanes])
                return jnp.concatenate([lo, mid, hi], axis=0)

            kk = jnp.concatenate([kv_rows(kp_ref, kc_ref, kn_ref), bias_t], axis=1)
            vv = jnp.concatenate(
                [kv_rows(vp_ref, vc_ref, vn_ref),
                 jnp.ones((3 * BLOCK, HEAD_DIM), _BF16)], axis=1)
            qs = jnp.concatenate(
                [q_ref[j * BLOCK:(j + 1) * BLOCK,
                       (g * GROUP + r) * HEAD_DIM:(g * GROUP + r + 1) * HEAD_DIM]
                 for r in range(GROUP)], axis=0)
            s = lax.dot_general(jnp.concatenate([qs, onehot], axis=1), kk,
                                (((1,), (1,)), ((), ())),
                                preferred_element_type=_F32)
            sink = jnp.concatenate(
                [jnp.full((BLOCK, HEAD_DIM), sink_ref[g * GROUP + r] * LOG2E, _F32)
                 for r in range(GROUP)], axis=0)
            sb = [s[:, i * BLOCK:(i + 1) * BLOCK] for i in range(3)]
            rowmax = jnp.max(jnp.maximum(jnp.maximum(sb[0], sb[1]), sb[2]),
                             axis=-1, keepdims=True)
            mx = jnp.maximum(jnp.broadcast_to(rowmax, (GROUP * BLOCK, HEAD_DIM)), sink)
            p = jnp.concatenate([jnp.exp2(t - mx) for t in sb], axis=1).astype(_BF16)
            o = jnp.dot(p, vv, preferred_element_type=_F32)
            denom = o[:, HEAD_DIM:] + jnp.exp2(sink - mx)
            o = o[:, :HEAD_DIM] * (1.0 / denom)
            for r in range(GROUP):
                cols = slice((g * GROUP + r) * HEAD_DIM, (g * GROUP + r + 1) * HEAD_DIM)
                gate = ga_ref[j * BLOCK:(j + 1) * BLOCK, cols].astype(_F32)
                o_ref[j * BLOCK:(j + 1) * BLOCK, cols] = (
                    o[r * BLOCK:(r + 1) * BLOCK, :] * gate).astype(_BF16)


def _attn(q, k, v, ga, sink, batch, seq):
    tq = TQ
    assert seq % tq == 0
    sub = tq // BLOCK
    nblk = seq // BLOCK
    q3 = q.reshape(batch, seq, ATTN_WIDTH)
    k3 = k.reshape(batch, seq, KV_WIDTH)
    v3 = v.reshape(batch, seq, KV_WIDTH)
    ga3 = ga.reshape(batch, seq, ATTN_WIDTH)
    cur = lambda b, n: (b, n, 0)
    prev = lambda b, n: (b, jnp.maximum(n * sub - 1, 0), 0)
    nxt = lambda b, n: (b, jnp.minimum(n * sub + sub, nblk - 1), 0)
    kv_specs = [
        pl.BlockSpec((None, BLOCK, KV_WIDTH), prev),
        pl.BlockSpec((None, tq, KV_WIDTH), cur),
        pl.BlockSpec((None, BLOCK, KV_WIDTH), nxt),
    ]
    out = pl.pallas_call(
        functools.partial(_attn_kernel, seq),
        grid=(batch, seq // tq),
        in_specs=[pl.BlockSpec(memory_space=pltpu.SMEM),
                  pl.BlockSpec((None, tq, ATTN_WIDTH), cur)]
                 + kv_specs + kv_specs
                 + [pl.BlockSpec((None, tq, ATTN_WIDTH), cur)],
        out_specs=pl.BlockSpec((None, tq, ATTN_WIDTH), cur),
        out_shape=jax.ShapeDtypeStruct((batch, seq, ATTN_WIDTH), _BF16),
        compiler_params=pltpu.CompilerParams(
            dimension_semantics=("arbitrary", "arbitrary"),
            vmem_limit_bytes=VMEM_LIMIT_BYTES),
        name="window_attn",
    )(sink, q3, k3, k3, k3, v3, v3, v3, ga3)
    return out.reshape(batch * seq, ATTN_WIDTH)


def _out_proj_kernel(a_ref, m_ref, x_ref, p_ref, wout_ref, postg_ref, wpe_ref, wpg_ref, o_ref):
    tm = x_ref.shape[0]
    subs = [slice(t * SUB_OUT, (t + 1) * SUB_OUT) for t in range(tm // SUB_OUT)]

    def mix_proj(sub):
        return (jnp.dot(a_ref[sub, :], wout_ref[:ATTN_WIDTH, :], preferred_element_type=_F32)
                + jnp.dot(m_ref[sub, :], wout_ref[ATTN_WIDTH:, :], preferred_element_type=_F32))

    def post_norm(sub, y):
        ms = jnp.mean(y * y, axis=-1, keepdims=True)
        return x_ref[sub, :] + y * lax.rsqrt(ms + EPS) * postg_ref[...]

    def ple(sub, x1):
        gate = _sigmoid(jnp.dot(x1.astype(_BF16), wpg_ref[...], preferred_element_type=_F32))
        pe = jnp.dot(p_ref[sub, :].astype(_BF16), wpe_ref[...], preferred_element_type=_F32)
        o_ref[sub, :] = x1 + gate * pe

    ys = [mix_proj(sub) for sub in subs]
    for sub, y in zip(subs, ys):
        ple(sub, post_norm(sub, y))


def _out_proj(a, m_gated, x2d, p2d, w_out_bf, post_g, w_pe_bf, w_pg_bf):
    m = x2d.shape[0]
    tm = TM_OUT
    assert m % tm == 0 and tm % SUB_OUT == 0
    row = lambda i: (i, 0)
    const2 = lambda i: (0, 0)
    resident = dict(pipeline_mode=pl.Buffered(1))
    return pl.pallas_call(
        _out_proj_kernel,
        grid=(m // tm,),
        in_specs=[
            pl.BlockSpec((tm, ATTN_WIDTH), row),
            pl.BlockSpec((tm, GMLP_WIDTH), row),
            pl.BlockSpec((tm, D_MODEL), row),
            pl.BlockSpec((tm, PLE_DIM), row),
            pl.BlockSpec((D_MODEL, D_MODEL), const2, **resident),
            pl.BlockSpec((1, D_MODEL), const2),
            pl.BlockSpec((PLE_DIM, D_MODEL), const2, **resident),
            pl.BlockSpec((D_MODEL, D_MODEL), const2, **resident),
        ],
        out_specs=pl.BlockSpec((tm, D_MODEL), row),
        out_shape=jax.ShapeDtypeStruct((m, D_MODEL), _F32),
        compiler_params=pltpu.CompilerParams(
            dimension_semantics=("arbitrary",),
            vmem_limit_bytes=VMEM_LIMIT_BYTES),
        name="out_proj",
    )(a, m_gated, x2d, p2d, w_out_bf, post_g, w_pe_bf, w_pg_bf)


def _rope_tables(seq):
    inv = 1.0 / (ROPE_THETA ** (jnp.arange(0, HEAD_DIM, 2, dtype=_F32) / HEAD_DIM))
    ang = jnp.arange(seq, dtype=_F32)[:, None] * inv[None, :]
    cos = jnp.cos(ang)
    sin = jnp.sin(ang)
    return (jnp.concatenate([cos, cos], axis=-1),
            jnp.concatenate([-sin, sin], axis=-1))


def _layer(x, p, params, tables):
    (pre_g, w_in_bf, sink, ln_g, ln_b, ws_bf, bs_b, w_out_bf, post_g, w_pe_bf, w_pg_bf) = params
    batch, seq, _ = x.shape
    cos_t, sin_t = tables
    x2d = x.reshape(batch * seq, D_MODEL)
    p2d = p.reshape(batch * seq, PLE_DIM)
    q, k, v, ga, m_gated = _in_proj(x2d, seq, pre_g, w_in_bf, cos_t[:seq], sin_t[:seq],
                                    ln_g, ln_b, ws_bf, bs_b)
    a = _attn(q, k, v, ga, sink, batch, seq)
    out = _out_proj(a, m_gated, x2d, p2d, w_out_bf, post_g, w_pe_bf, w_pg_bf)
    return out.reshape(batch, seq, D_MODEL)


def kernel(x_prompt, x_sample, p_prompt, p_sample, pre_norm_g, w_in, attn_sink,
           gmlp_ln_g, gmlp_ln_b, gmlp_ws, gmlp_bs, w_out, post_norm_g, w_pe, w_pg):
    depth = w_in.shape[0]
    tables = _rope_tables(max(x_prompt.shape[1], x_sample.shape[1]))
    y_prompt, y_sample = x_prompt, x_sample
    for i in range(depth):
        params = (
            pre_norm_g[i].reshape(1, D_MODEL),
            w_in[i].astype(_BF16),
            attn_sink[i],
            gmlp_ln_g[i].reshape(1, GMLP_WIDTH),
            gmlp_ln_b[i].reshape(1, GMLP_WIDTH),
            gmlp_ws[i].astype(_BF16),
            jnp.broadcast_to(gmlp_bs[i][:, :, None], (N_GMLP_HEADS, CHUNK, HEAD_DIM)),
            w_out[i].astype(_BF16),
            post_norm_g[i].reshape(1, D_MODEL),
            w_pe[i].astype(_BF16),
            w_pg[i].astype(_BF16),
        )
        y_prompt = _layer(y_prompt, p_prompt[i], params, tables)
        y_sample = _layer(y_sample, p_sample[i], params, tables)
    return (y_prompt, y_sample)
```

```python
import functools
import math

import jax
import jax.numpy as jnp
from jax import lax
from jax.experimental import pallas as pl
from jax.experimental.pallas import tpu as pltpu

D_MODEL = 2048
HEAD_DIM = 128
ATTN_WIDTH = 1024
N_HEADS = 8
N_KV = 2
GROUP = 4
KV_WIDTH = 256
WINDOW = 128
BLOCK = 128
ROPE_THETA = 10000.0
GMLP_WIDTH = 1024
N_GMLP_HEADS = 8
CHUNK = 128
PLE_DIM = 256
EPS = 1e-6
IN_WIDTH = 5632
NEG_INF = -1e30
LOG2E = math.log2(math.e)
Q_SCALE = HEAD_DIM ** -0.5 * LOG2E

VMEM_LIMIT_BYTES = 56 * 1024 * 1024

TM_IN = 512
SUB_IN = 256
NCHUNK = 1024
TQ = 2048
TM_OUT = 512
SUB_OUT = 256

_BF16 = jnp.bfloat16
_F32 = jnp.float32


def _sigmoid(x):
    return 1.0 / (1.0 + jnp.exp(-x))


def _gelu_exact(x):
    return 0.5 * x * (1.0 + lax.erf(x * (2.0 ** -0.5)))


def _in_proj_kernel(x_ref, g_ref, w_ref, rope_row_ref, rope_base_ref, lng_ref, lnb_ref, ws_ref,
                    bsb_ref,
                    q_ref, k_ref, v_ref, ga_ref, m_ref, h_ref, vgf_ref, vn_ref):
    tm = x_ref.shape[0]
    rows = 64
    for r in range(tm // rows):
        x = x_ref[r * rows:(r + 1) * rows, :]
        ms = jnp.mean(x * x, axis=-1, keepdims=True)
        h = x * lax.rsqrt(ms + EPS) * g_ref[...]
        h_ref[r * rows:(r + 1) * rows, :] = h.astype(_BF16)

    cos_r, sin_r = rope_row_ref[0], rope_row_ref[1]
    cos_b, sin_b, sign = rope_base_ref[0:1, :], rope_base_ref[1:2, :], rope_base_ref[2:3, :]
    cos = cos_r * cos_b - sin_r * sin_b
    sin = (sin_r * cos_b + cos_r * sin_b) * sign

    def rope(t):
        return t * cos + pltpu.roll(t, HEAD_DIM // 2, 1) * sin

    def proj(start, width):
        return jnp.concatenate(
            [jnp.dot(h_ref[r * SUB_IN:(r + 1) * SUB_IN, :], w_ref[:, start:start + width],
                     preferred_element_type=_F32) for r in range(tm // SUB_IN)], axis=0)

    q0, k0, ga0 = 0, ATTN_WIDTH, ATTN_WIDTH + 2 * KV_WIDTH
    u0 = ga0 + ATTN_WIDTH
    vg0 = u0 + GMLP_WIDTH
    gg0 = vg0 + GMLP_WIDTH
    nsplit = ATTN_WIDTH // NCHUNK
    heads_per_chunk = NCHUNK // HEAD_DIM
    nchunks = tm // CHUNK
    mix_ref = vgf_ref

    def gmlp_v():
        for c in range(nsplit):
            ccols = slice(c * NCHUNK, (c + 1) * NCHUNK)
            vgf_ref[:, ccols] = _gelu_exact(proj(vg0 + c * NCHUNK, NCHUNK))

    def layer_norm():
        for r in range(tm // rows):
            rs = slice(r * rows, (r + 1) * rows)
            vf = vgf_ref[rs, :]
            mu = jnp.mean(vf, axis=-1, keepdims=True)
            vc = vf - mu
            var = jnp.mean(vc * vc, axis=-1, keepdims=True)
            vn = vc * lax.rsqrt(var + EPS) * lng_ref[...] + lnb_ref[...]
            vn_ref[rs, :] = vn.astype(_BF16)

    def spatial_mix():
        for h in range(N_GMLP_HEADS):
            cols = slice(h * HEAD_DIM, (h + 1) * HEAD_DIM)
            rhs = jnp.concatenate(
                [vn_ref[rc * CHUNK:(rc + 1) * CHUNK, cols] for rc in range(nchunks)], axis=1)
            mixed = jnp.dot(ws_ref[h], rhs, preferred_element_type=_F32)
            for rc in range(nchunks):
                mix_ref[rc * CHUNK:(rc + 1) * CHUNK, cols] = (
                    mixed[:, rc * HEAD_DIM:(rc + 1) * HEAD_DIM] + bsb_ref[h])

    def gmlp_gate():
        for c in range(nsplit):
            ccols = slice(c * NCHUNK, (c + 1) * NCHUNK)
            gu = _gelu_exact(proj(u0 + c * NCHUNK, NCHUNK))
            acc = proj(gg0 + c * NCHUNK, NCHUNK)
            m_ref[:, ccols] = (gu * mix_ref[:, ccols] * (acc * _sigmoid(acc))).astype(_BF16)

    def attn_q():
        for c in range(nsplit):
            acc = proj(q0 + c * NCHUNK, NCHUNK)
            for hd in range(heads_per_chunk):
                col = c * NCHUNK + hd * HEAD_DIM
                q_ref[:, col:col + HEAD_DIM] = (rope(
                    acc[:, hd * HEAD_DIM:(hd + 1) * HEAD_DIM]) * Q_SCALE).astype(_BF16)

    def attn_gate():
        for c in range(nsplit):
            acc = proj(ga0 + c * NCHUNK, NCHUNK)
            ga_ref[:, c * NCHUNK:(c + 1) * NCHUNK] = (acc * _sigmoid(acc)).astype(_BF16)

    def attn_kv():
        acc = proj(k0, KV_WIDTH)
        for hd in range(N_KV):
            k_ref[:, hd * HEAD_DIM:(hd + 1) * HEAD_DIM] = rope(
                acc[:, hd * HEAD_DIM:(hd + 1) * HEAD_DIM]).astype(_BF16)
        v_ref[...] = proj(k0 + KV_WIDTH, KV_WIDTH).astype(_BF16)

    gmlp_v()
    attn_q()
    layer_norm()
    spatial_mix()
    gmlp_gate()
    attn_gate()
    attn_kv()


def _in_proj(x2d, seq, pre_g, w_in_bf, rope_row, rope_base, ln_g, ln_b, ws_bf, bs_b):
    m = x2d.shape[0]
    tm = TM_IN
    assert m % tm == 0 and seq % tm == 0 and tm % CHUNK == 0
    blocks_per_seq = seq // tm
    const = lambda i: (0, 0)
    const3 = lambda i: (0, 0, 0)
    row = lambda i: (i, 0)
    resident = dict(pipeline_mode=pl.Buffered(1))
    out_widths = (ATTN_WIDTH, KV_WIDTH, KV_WIDTH, ATTN_WIDTH, GMLP_WIDTH)
    return pl.pallas_call(
        _in_proj_kernel,
        grid=(m // tm,),
        in_specs=[
            pl.BlockSpec((tm, D_MODEL), row),
            pl.BlockSpec((1, D_MODEL), const),
            pl.BlockSpec((D_MODEL, IN_WIDTH), const, **resident),
            pl.BlockSpec((2, tm, HEAD_DIM), const3),
            pl.BlockSpec((None, 8, HEAD_DIM), lambda i: (i % blocks_per_seq, 0, 0)),
            pl.BlockSpec((1, GMLP_WIDTH), const),
            pl.BlockSpec((1, GMLP_WIDTH), const),
            pl.BlockSpec((N_GMLP_HEADS, CHUNK, CHUNK), const3, **resident),
            pl.BlockSpec((N_GMLP_HEADS, CHUNK, HEAD_DIM), const3, **resident),
        ],
        out_specs=[pl.BlockSpec((tm, w), row) for w in out_widths],
        out_shape=[jax.ShapeDtypeStruct((m, w), _BF16) for w in out_widths],
        scratch_shapes=[pltpu.VMEM((tm, D_MODEL), _BF16),
                        pltpu.VMEM((tm, GMLP_WIDTH), _F32),
                        pltpu.VMEM((tm, GMLP_WIDTH), _BF16)],
        compiler_params=pltpu.CompilerParams(
            dimension_semantics=("arbitrary",),
            vmem_limit_bytes=VMEM_LIMIT_BYTES),
        name="in_proj",
    )(x2d, pre_g, w_in_bf, rope_row, rope_base, ln_g, ln_b, ws_bf, bs_b)


def _attn_kernel(seq, sink_ref, q_ref, kp_ref, kc_ref, kn_ref,
                 vp_ref, vc_ref, vn_ref, ga_ref, o_ref):
    n = pl.program_id(1)
    tq = q_ref.shape[0]
    nsub = tq // BLOCK

    ik = lax.broadcasted_iota(jnp.int32, (3 * BLOCK, BLOCK), 0)
    iq = lax.broadcasted_iota(jnp.int32, (3 * BLOCK, BLOCK), 1)
    band_t = jnp.abs(iq + BLOCK - ik) <= WINDOW
    eye = (lax.broadcasted_iota(jnp.int32, (BLOCK, BLOCK), 0)
           == lax.broadcasted_iota(jnp.int32, (BLOCK, BLOCK), 1)).astype(_BF16)
    onehot = jnp.concatenate([eye] * GROUP, axis=0)

    for j in range(nsub):
        kbase = n * tq + (j - 1) * BLOCK
        in_seq = (ik + kbase >= 0) & (ik + kbase < seq)
        bias_t = jnp.where(band_t & in_seq, 0.0, NEG_INF).astype(_BF16)
        for g in range(N_KV):
            lanes = slice(g * HEAD_DIM, (g + 1) * HEAD_DIM)

            def kv_rows(prev_ref, cur_ref, next_ref):
                lo = prev_ref[:, lanes] if j == 0 else cur_ref[(j - 1) * BLOCK:j * BLOCK, lanes]
                mid = cur_ref[j * BLOCK:(j + 1) * BLOCK, lanes]
                hi = (next_ref[:, lanes] if j == nsub - 1
                      else cur_ref[(j + 1) * BLOCK:(j + 2) * BLOCK, lanes])
                return jnp.concatenate([lo, mid, hi], axis=0)

            kk = jnp.concatenate([kv_rows(kp_ref, kc_ref, kn_ref), bias_t], axis=1)
            vv = jnp.concatenate(
                [kv_rows(vp_ref, vc_ref, vn_ref),
                 jnp.ones((3 * BLOCK, HEAD_DIM), _BF16)], axis=1)
            qs = jnp.concatenate(
                [q_ref[j * BLOCK:(j + 1) * BLOCK,
                       (g * GROUP + r) * HEAD_DIM:(g * GROUP + r + 1) * HEAD_DIM]
                 for r in range(GROUP)], axis=0)
            s = lax.dot_general(jnp.concatenate([qs, onehot], axis=1), kk,
                                (((1,), (1,)), ((), ())),
                                preferred_element_type=_F32)
            sink = jnp.concatenate(
                [jnp.full((BLOCK, HEAD_DIM), sink_ref[g * GROUP + r] * LOG2E, _F32)
                 for r in range(GROUP)], axis=0)
            sb = [s[:, i * BLOCK:(i + 1) * BLOCK] for i in range(3)]
            rowmax = jnp.max(jnp.maximum(jnp.maximum(sb[0], sb[1]), sb[2]),
                             axis=-1, keepdims=True)
            mx = jnp.maximum(jnp.broadcast_to(rowmax, (GROUP * BLOCK, HEAD_DIM)), sink)
            p = jnp.concatenate([jnp.exp2(t - mx) for t in sb], axis=1).astype(_BF16)
            o = jnp.dot(p, vv, preferred_element_type=_F32)
            denom = o[:, HEAD_DIM:] + jnp.exp2(sink - mx)
            o = o[:, :HEAD_DIM] * (1.0 / denom)
            for r in range(GROUP):
                cols = slice((g * GROUP + r) * HEAD_DIM, (g * GROUP + r + 1) * HEAD_DIM)
                gate = ga_ref[j * BLOCK:(j + 1) * BLOCK, cols].astype(_F32)
                o_ref[j * BLOCK:(j + 1) * BLOCK, cols] = (
                    o[r * BLOCK:(r + 1) * BLOCK, :] * gate).astype(_BF16)


def _attn(q, k, v, ga, sink, batch, seq):
    tq = TQ
    assert seq % tq == 0
    sub = tq // BLOCK
    nblk = seq // BLOCK
    q3 = q.reshape(batch, seq, ATTN_WIDTH)
    k3 = k.reshape(batch, seq, KV_WIDTH)
    v3 = v.reshape(batch, seq, KV_WIDTH)
    ga3 = ga.reshape(batch, seq, ATTN_WIDTH)
    cur = lambda b, n: (b, n, 0)
    prev = lambda b, n: (b, jnp.maximum(n * sub - 1, 0), 0)
    nxt = lambda b, n: (b, jnp.minimum(n * sub + sub, nblk - 1), 0)
    kv_specs = [
        pl.BlockSpec((None, BLOCK, KV_WIDTH), prev),
        pl.BlockSpec((None, tq, KV_WIDTH), cur),
        pl.BlockSpec((None, BLOCK, KV_WIDTH), nxt),
    ]
    out = pl.pallas_call(
        functools.partial(_attn_kernel, seq),
        grid=(batch, seq // tq),
        in_specs=[pl.BlockSpec(memory_space=pltpu.SMEM),
                  pl.BlockSpec((None, tq, ATTN_WIDTH), cur)]
                 + kv_specs + kv_specs
                 + [pl.BlockSpec((None, tq, ATTN_WIDTH), cur)],
        out_specs=pl.BlockSpec((None, tq, ATTN_WIDTH), cur),
        out_shape=jax.ShapeDtypeStruct((batch, seq, ATTN_WIDTH), _BF16),
        compiler_params=pltpu.CompilerParams(
            dimension_semantics=("arbitrary", "arbitrary"),
            vmem_limit_bytes=VMEM_LIMIT_BYTES),
        name="window_attn",
    )(sink, q3, k3, k3, k3, v3, v3, v3, ga3)
    return out.reshape(batch * seq, ATTN_WIDTH)


def _out_proj_kernel(a_ref, m_ref, x_ref, p_ref, wout_ref, postg_ref, wpe_ref, wpg_ref, o_ref):
    tm = x_ref.shape[0]
    subs = [slice(t * SUB_OUT, (t + 1) * SUB_OUT) for t in range(tm // SUB_OUT)]

    def mix_proj(sub):
        am = jnp.concatenate([a_ref[sub, :], m_ref[sub, :]], axis=1)
        return jnp.dot(am, wout_ref[...], preferred_element_type=_F32)

    def post_norm(sub, y):
        ms = jnp.mean(y * y, axis=-1, keepdims=True)
        return x_ref[sub, :] + y * lax.rsqrt(ms + EPS) * postg_ref[...]

    def ple(sub, x1):
        gate = _sigmoid(jnp.dot(x1.astype(_BF16), wpg_ref[...], preferred_element_type=_F32))
        pe = jnp.dot(p_ref[sub, :].astype(_BF16), wpe_ref[...], preferred_element_type=_F32)
        o_ref[sub, :] = x1 + gate * pe

    ys = [mix_proj(sub) for sub in subs]
    for sub, y in zip(subs, ys):
        ple(sub, post_norm(sub, y))


def _out_proj(a, m_gated, x2d, p2d, w_out_bf, post_g, w_pe_bf, w_pg_bf):
    m = x2d.shape[0]
    tm = TM_OUT
    assert m % tm == 0 and tm % SUB_OUT == 0
    row = lambda i: (i, 0)
    const2 = lambda i: (0, 0)
    resident = dict(pipeline_mode=pl.Buffered(1))
    return pl.pallas_call(
        _out_proj_kernel,
        grid=(m // tm,),
        in_specs=[
            pl.BlockSpec((tm, ATTN_WIDTH), row),
            pl.BlockSpec((tm, GMLP_WIDTH), row),
            pl.BlockSpec((tm, D_MODEL), row),
            pl.BlockSpec((tm, PLE_DIM), row),
            pl.BlockSpec((D_MODEL, D_MODEL), const2, **resident),
            pl.BlockSpec((1, D_MODEL), const2),
            pl.BlockSpec((PLE_DIM, D_MODEL), const2, **resident),
            pl.BlockSpec((D_MODEL, D_MODEL), const2, **resident),
        ],
        out_specs=pl.BlockSpec((tm, D_MODEL), row),
        out_shape=jax.ShapeDtypeStruct((m, D_MODEL), _F32),
        compiler_params=pltpu.CompilerParams(
            dimension_semantics=("arbitrary",),
            vmem_limit_bytes=VMEM_LIMIT_BYTES),
        name="out_proj",
    )(a, m_gated, x2d, p2d, w_out_bf, post_g, w_pe_bf, w_pg_bf)


def _rope_tables(max_seq, tm):
    inv = 1.0 / (ROPE_THETA ** (jnp.arange(0, HEAD_DIM, 2, dtype=_F32) / HEAD_DIM))
    inv2 = jnp.concatenate([inv, inv])[None, :]
    ang_row = jnp.arange(tm, dtype=_F32)[:, None] * inv2
    ang_base = jnp.arange(0, max_seq, tm, dtype=_F32)[:, None] * inv2
    nb = max_seq // tm
    sign = jnp.broadcast_to(
        jnp.where(jnp.arange(HEAD_DIM) < HEAD_DIM // 2, -1.0, 1.0).astype(_F32)[None, :],
        (nb, HEAD_DIM))
    rope_row = jnp.stack([jnp.cos(ang_row), jnp.sin(ang_row)])
    rope_base = jnp.stack([jnp.cos(ang_base), jnp.sin(ang_base), sign]
                          + [jnp.zeros((nb, HEAD_DIM), _F32)] * 5, axis=1)
    return rope_row, rope_base


def _layer(x, p, params, tables):
    (pre_g, w_in_bf, sink, ln_g, ln_b, ws_bf, bs_b, w_out_bf, post_g, w_pe_bf, w_pg_bf) = params
    batch, seq, _ = x.shape
    rope_row, rope_base = tables
    x2d = x.reshape(batch * seq, D_MODEL)
    p2d = p.reshape(batch * seq, PLE_DIM)
    q, k, v, ga, m_gated = _in_proj(x2d, seq, pre_g, w_in_bf, rope_row, rope_base,
                                    ln_g, ln_b, ws_bf, bs_b)
    a = _attn(q, k, v, ga, sink, batch, seq)
    out = _out_proj(a, m_gated, x2d, p2d, w_out_bf, post_g, w_pe_bf, w_pg_bf)
    return out.reshape(batch, seq, D_MODEL)


def kernel(x_prompt, x_sample, p_prompt, p_sample, pre_norm_g, w_in, attn_sink,
           gmlp_ln_g, gmlp_ln_b, gmlp_ws, gmlp_bs, w_out, post_norm_g, w_pe, w_pg):
    depth = w_in.shape[0]
    tables = _rope_tables(max(x_prompt.shape[1], x_sample.shape[1]), TM_IN)
    y_prompt, y_sample = x_prompt, x_sample
    for i in range(depth):
        params = (
            pre_norm_g[i].reshape(1, D_MODEL),
            w_in[i].astype(_BF16),
            attn_sink[i],
            gmlp_ln_g[i].reshape(1, GMLP_WIDTH),
            gmlp_ln_b[i].reshape(1, GMLP_WIDTH),
            gmlp_ws[i].astype(_BF16),
            jnp.broadcast_to(gmlp_bs[i][:, :, None], (N_GMLP_HEADS, CHUNK, HEAD_DIM)),
            w_out[i].astype(_BF16),
            post_norm_g[i].reshape(1, D_MODEL),
            w_pe[i].astype(_BF16),
            w_pg[i].astype(_BF16),
        )
        y_prompt = _layer(y_prompt, p_prompt[i], params, tables)
        y_sample = _layer(y_sample, p_sample[i], params, tables)
    return (y_prompt, y_sample)
```

```python
import functools
import math

import jax
import jax.numpy as jnp
from jax import lax
from jax.experimental import pallas as pl
from jax.experimental.pallas import tpu as pltpu

D_MODEL = 2048
HEAD_DIM = 128
ATTN_WIDTH = 1024
N_HEADS = 8
N_KV = 2
GROUP = 4
KV_WIDTH = 256
WINDOW = 128
BLOCK = 128
ROPE_THETA = 10000.0
GMLP_WIDTH = 1024
N_GMLP_HEADS = 8
CHUNK = 128
PLE_DIM = 256
EPS = 1e-6
IN_WIDTH = 5632
NEG_INF = -1e30
LOG2E = math.log2(math.e)
Q_SCALE = HEAD_DIM ** -0.5 * LOG2E

VMEM_LIMIT_BYTES = 56 * 1024 * 1024

TM_IN = 512
SUB_IN = 256
NCHUNK = 1024
TQ = 2048
TM_OUT = 512
SUB_OUT = 256

_BF16 = jnp.bfloat16
_F32 = jnp.float32


def _sigmoid(x):
    return 1.0 / (1.0 + jnp.exp(-x))


def _gelu_exact(x):
    return 0.5 * x * (1.0 + lax.erf(x * (2.0 ** -0.5)))


def _in_proj_kernel(x_ref, g_ref, w_ref, rope_row_ref, rope_base_ref, lng_ref, lnb_ref, ws_ref,
                    bsb_ref,
                    q_ref, k_ref, v_ref, ga_ref, m_ref, h_ref, vgf_ref, vn_ref):
    tm = x_ref.shape[0]
    rows = 64
    for r in range(tm // rows):
        x = x_ref[r * rows:(r + 1) * rows, :]
        ms = jnp.mean(x * x, axis=-1, keepdims=True)
        h = x * lax.rsqrt(ms + EPS) * g_ref[...]
        h_ref[r * rows:(r + 1) * rows, :] = h.astype(_BF16)

    cos_r, sin_r = rope_row_ref[0], rope_row_ref[1]
    base = rope_base_ref[pl.program_id(0) % rope_base_ref.shape[0]]
    cos_b, sin_b, sign = base[0:1, :], base[1:2, :], base[2:3, :]
    cos = cos_r * cos_b - sin_r * sin_b
    sin = (sin_r * cos_b + cos_r * sin_b) * sign

    def rope(t):
        return t * cos + pltpu.roll(t, HEAD_DIM // 2, 1) * sin

    def proj(start, width):
        return jnp.concatenate(
            [jnp.dot(h_ref[r * SUB_IN:(r + 1) * SUB_IN, :], w_ref[:, start:start + width],
                     preferred_element_type=_F32) for r in range(tm // SUB_IN)], axis=0)

    q0, k0, ga0 = 0, ATTN_WIDTH, ATTN_WIDTH + 2 * KV_WIDTH
    u0 = ga0 + ATTN_WIDTH
    vg0 = u0 + GMLP_WIDTH
    gg0 = vg0 + GMLP_WIDTH
    nsplit = ATTN_WIDTH // NCHUNK
    heads_per_chunk = NCHUNK // HEAD_DIM
    nchunks = tm // CHUNK
    mix_ref = vgf_ref

    def gmlp_v():
        for c in range(nsplit):
            ccols = slice(c * NCHUNK, (c + 1) * NCHUNK)
            vgf_ref[:, ccols] = _gelu_exact(proj(vg0 + c * NCHUNK, NCHUNK))

    def layer_norm():
        for r in range(tm // rows):
            rs = slice(r * rows, (r + 1) * rows)
            vf = vgf_ref[rs, :]
            mu = jnp.mean(vf, axis=-1, keepdims=True)
            vc = vf - mu
            var = jnp.mean(vc * vc, axis=-1, keepdims=True)
            vn = vc * lax.rsqrt(var + EPS) * lng_ref[...] + lnb_ref[...]
            vn_ref[rs, :] = vn.astype(_BF16)

    def spatial_mix():
        for h in range(N_GMLP_HEADS):
            cols = slice(h * HEAD_DIM, (h + 1) * HEAD_DIM)
            rhs = jnp.concatenate(
                [vn_ref[rc * CHUNK:(rc + 1) * CHUNK, cols] for rc in range(nchunks)], axis=1)
            mixed = jnp.dot(ws_ref[h], rhs, preferred_element_type=_F32)
            for rc in range(nchunks):
                mix_ref[rc * CHUNK:(rc + 1) * CHUNK, cols] = (
                    mixed[:, rc * HEAD_DIM:(rc + 1) * HEAD_DIM] + bsb_ref[h])

    def gmlp_gate():
        for c in range(nsplit):
            ccols = slice(c * NCHUNK, (c + 1) * NCHUNK)
            gu = _gelu_exact(proj(u0 + c * NCHUNK, NCHUNK))
            acc = proj(gg0 + c * NCHUNK, NCHUNK)
            m_ref[:, ccols] = (gu * mix_ref[:, ccols] * (acc * _sigmoid(acc))).astype(_BF16)

    def attn_q():
        for c in range(nsplit):
            acc = proj(q0 + c * NCHUNK, NCHUNK)
            for hd in range(heads_per_chunk):
                col = c * NCHUNK + hd * HEAD_DIM
                q_ref[:, col:col + HEAD_DIM] = (rope(
                    acc[:, hd * HEAD_DIM:(hd + 1) * HEAD_DIM]) * Q_SCALE).astype(_BF16)

    def attn_gate():
        for c in range(nsplit):
            acc = proj(ga0 + c * NCHUNK, NCHUNK)
            ga_ref[:, c * NCHUNK:(c + 1) * NCHUNK] = (acc * _sigmoid(acc)).astype(_BF16)

    def attn_kv():
        acc = proj(k0, KV_WIDTH)
        for hd in range(N_KV):
            k_ref[:, hd * HEAD_DIM:(hd + 1) * HEAD_DIM] = rope(
                acc[:, hd * HEAD_DIM:(hd + 1) * HEAD_DIM]).astype(_BF16)
        v_ref[...] = proj(k0 + KV_WIDTH, KV_WIDTH).astype(_BF16)

    gmlp_v()
    attn_q()
    layer_norm()
    spatial_mix()
    gmlp_gate()
    attn_gate()
    attn_kv()


def _in_proj(x2d, seq, pre_g, w_in_bf, rope_row, rope_base, ln_g, ln_b, ws_bf, bs_b):
    m = x2d.shape[0]
    tm = TM_IN
    assert m % tm == 0 and seq % tm == 0 and tm % CHUNK == 0
    blocks_per_seq = seq // tm
    const = lambda i: (0, 0)
    const3 = lambda i: (0, 0, 0)
    row = lambda i: (i, 0)
    resident = dict(pipeline_mode=pl.Buffered(1))
    out_widths = (ATTN_WIDTH, KV_WIDTH, KV_WIDTH, ATTN_WIDTH, GMLP_WIDTH)
    return pl.pallas_call(
        _in_proj_kernel,
        grid=(m // tm,),
        in_specs=[
            pl.BlockSpec((tm, D_MODEL), row),
            pl.BlockSpec((1, D_MODEL), const),
            pl.BlockSpec((D_MODEL, IN_WIDTH), const, **resident),
            pl.BlockSpec((2, tm, HEAD_DIM), const3),
            pl.BlockSpec((blocks_per_seq, 8, HEAD_DIM), const3),
            pl.BlockSpec((1, GMLP_WIDTH), const),
            pl.BlockSpec((1, GMLP_WIDTH), const),
            pl.BlockSpec((N_GMLP_HEADS, CHUNK, CHUNK), const3, **resident),
            pl.BlockSpec((N_GMLP_HEADS, CHUNK, HEAD_DIM), const3, **resident),
        ],
        out_specs=[pl.BlockSpec((tm, w), row) for w in out_widths],
        out_shape=[jax.ShapeDtypeStruct((m, w), _BF16) for w in out_widths],
        scratch_shapes=[pltpu.VMEM((tm, D_MODEL), _BF16),
                        pltpu.VMEM((tm, GMLP_WIDTH), _F32),
                        pltpu.VMEM((tm, GMLP_WIDTH), _BF16)],
        compiler_params=pltpu.CompilerParams(
            dimension_semantics=("arbitrary",),
            vmem_limit_bytes=VMEM_LIMIT_BYTES),
        name="in_proj",
    )(x2d, pre_g, w_in_bf, rope_row, rope_base, ln_g, ln_b, ws_bf, bs_b)


def _attn_kernel(seq, ncast, sink_ref, q_ref, kp_ref, kc_ref, kn_ref,
                 vp_ref, vc_ref, vn_ref, ga_ref, *rest):
    o_ref = rest[ncast]
    for src_ref, dst_ref in zip(rest[:ncast], rest[ncast + 1:]):
        dst_ref[...] = src_ref[...].astype(_BF16)

    n = pl.program_id(1)
    tq = q_ref.shape[0]
    nsub = tq // BLOCK

    ik = lax.broadcasted_iota(jnp.int32, (3 * BLOCK, BLOCK), 0)
    iq = lax.broadcasted_iota(jnp.int32, (3 * BLOCK, BLOCK), 1)
    band_t = jnp.abs(iq + BLOCK - ik) <= WINDOW
    eye = (lax.broadcasted_iota(jnp.int32, (BLOCK, BLOCK), 0)
           == lax.broadcasted_iota(jnp.int32, (BLOCK, BLOCK), 1)).astype(_BF16)
    onehot = jnp.concatenate([eye] * GROUP, axis=0)

    for j in range(nsub):
        kbase = n * tq + (j - 1) * BLOCK
        in_seq = (ik + kbase >= 0) & (ik + kbase < seq)
        bias_t = jnp.where(band_t & in_seq, 0.0, NEG_INF).astype(_BF16)
        for g in range(N_KV):
            lanes = slice(g * HEAD_DIM, (g + 1) * HEAD_DIM)

            def kv_rows(prev_ref, cur_ref, next_ref):
                lo = prev_ref[:, lanes] if j == 0 else cur_ref[(j - 1) * BLOCK:j * BLOCK, lanes]
                mid = cur_ref[j * BLOCK:(j + 1) * BLOCK, lanes]
                hi = (next_ref[:, lanes] if j == nsub - 1
                      else cur_ref[(j + 1) * BLOCK:(j + 2) * BLOCK, lanes])
                return jnp.concatenate([lo, mid, hi], axis=0)

            kk = jnp.concatenate([kv_rows(kp_ref, kc_ref, kn_ref), bias_t], axis=1)
            vv = jnp.concatenate(
                [kv_rows(vp_ref, vc_ref, vn_ref),
                 jnp.ones((3 * BLOCK, HEAD_DIM), _BF16)], axis=1)
            qs = jnp.concatenate(
                [q_ref[j * BLOCK:(j + 1) * BLOCK,
                       (g * GROUP + r) * HEAD_DIM:(g * GROUP + r + 1) * HEAD_DIM]
                 for r in range(GROUP)], axis=0)
            s = lax.dot_general(jnp.concatenate([qs, onehot], axis=1), kk,
                                (((1,), (1,)), ((), ())),
                                preferred_element_type=_F32)
            sink = jnp.concatenate(
                [jnp.full((BLOCK, HEAD_DIM), sink_ref[g * GROUP + r] * LOG2E, _F32)
                 for r in range(GROUP)], axis=0)
            sb = [s[:, i * BLOCK:(i + 1) * BLOCK] for i in range(3)]
            rowmax = jnp.max(jnp.maximum(jnp.maximum(sb[0], sb[1]), sb[2]),
                             axis=-1, keepdims=True)
            mx = jnp.maximum(jnp.broadcast_to(rowmax, (GROUP * BLOCK, HEAD_DIM)), sink)
            p = jnp.concatenate([jnp.exp2(t - mx) for t in sb], axis=1).astype(_BF16)
            o = jnp.dot(p, vv, preferred_element_type=_F32)
            denom = o[:, HEAD_DIM:] + jnp.exp2(sink - mx)
            o = o[:, :HEAD_DIM] * (1.0 / denom)
            for r in range(GROUP):
                cols = slice((g * GROUP + r) * HEAD_DIM, (g * GROUP + r + 1) * HEAD_DIM)
                gate = ga_ref[j * BLOCK:(j + 1) * BLOCK, cols].astype(_F32)
                o_ref[j * BLOCK:(j + 1) * BLOCK, cols] = (
                    o[r * BLOCK:(r + 1) * BLOCK, :] * gate).astype(_BF16)


def _attn(q, k, v, ga, sink, batch, seq, cast_weights=()):
    tq = TQ
    assert seq % tq == 0
    sub = tq // BLOCK
    nblk = seq // BLOCK
    nq = seq // tq
    nsteps = batch * nq
    slab = lambda b, n: (b * nq + n, 0)
    cast_specs = []
    for w in cast_weights:
        rows = w.shape[0] // nsteps
        assert w.shape[0] % nsteps == 0 and rows % 16 == 0
        cast_specs.append(pl.BlockSpec((rows, w.shape[1]), slab))
    q3 = q.reshape(batch, seq, ATTN_WIDTH)
    k3 = k.reshape(batch, seq, KV_WIDTH)
    v3 = v.reshape(batch, seq, KV_WIDTH)
    ga3 = ga.reshape(batch, seq, ATTN_WIDTH)
    cur = lambda b, n: (b, n, 0)
    prev = lambda b, n: (b, jnp.maximum(n * sub - 1, 0), 0)
    nxt = lambda b, n: (b, jnp.minimum(n * sub + sub, nblk - 1), 0)
    kv_specs = [
        pl.BlockSpec((None, BLOCK, KV_WIDTH), prev),
        pl.BlockSpec((None, tq, KV_WIDTH), cur),
        pl.BlockSpec((None, BLOCK, KV_WIDTH), nxt),
    ]
    outs = pl.pallas_call(
        functools.partial(_attn_kernel, seq, len(cast_weights)),
        grid=(batch, nq),
        in_specs=[pl.BlockSpec(memory_space=pltpu.SMEM),
                  pl.BlockSpec((None, tq, ATTN_WIDTH), cur)]
                 + kv_specs + kv_specs
                 + [pl.BlockSpec((None, tq, ATTN_WIDTH), cur)]
                 + cast_specs,
        out_specs=[pl.BlockSpec((None, tq, ATTN_WIDTH), cur)] + cast_specs,
        out_shape=[jax.ShapeDtypeStruct((batch, seq, ATTN_WIDTH), _BF16)]
                  + [jax.ShapeDtypeStruct(w.shape, _BF16) for w in cast_weights],
        compiler_params=pltpu.CompilerParams(
            dimension_semantics=("arbitrary", "arbitrary"),
            vmem_limit_bytes=VMEM_LIMIT_BYTES),
        name="window_attn",
    )(sink, q3, k3, k3, k3, v3, v3, v3, ga3, *cast_weights)
    return (outs[0].reshape(batch * seq, ATTN_WIDTH), *outs[1:])


def _out_proj_kernel(a_ref, m_ref, x_ref, p_ref, wout_ref, postg_ref, wpe_ref, wpg_ref, o_ref):
    tm = x_ref.shape[0]
    subs = [slice(t * SUB_OUT, (t + 1) * SUB_OUT) for t in range(tm // SUB_OUT)]

    def mix_proj(sub):
        am = jnp.concatenate([a_ref[sub, :], m_ref[sub, :]], axis=1)
        return jnp.dot(am, wout_ref[...], preferred_element_type=_F32)

    def post_norm(sub, y):
        ms = jnp.mean(y * y, axis=-1, keepdims=True)
        return x_ref[sub, :] + y * lax.rsqrt(ms + EPS) * postg_ref[...]

    def ple(sub, x1):
        gate = _sigmoid(jnp.dot(x1.astype(_BF16), wpg_ref[...], preferred_element_type=_F32))
        pe = jnp.dot(p_ref[sub, :].astype(_BF16), wpe_ref[...], preferred_element_type=_F32)
        o_ref[sub, :] = x1 + gate * pe

    ys = [mix_proj(sub) for sub in subs]
    for sub, y in zip(subs, ys):
        ple(sub, post_norm(sub, y))


def _out_proj(a, m_gated, x2d, p2d, w_out_bf, post_g, w_pe_bf, w_pg_bf):
    m = x2d.shape[0]
    tm = TM_OUT
    assert m % tm == 0 and tm % SUB_OUT == 0
    row = lambda i: (i, 0)
    const2 = lambda i: (0, 0)
    resident = dict(pipeline_mode=pl.Buffered(1))
    return pl.pallas_call(
        _out_proj_kernel,
        grid=(m // tm,),
        in_specs=[
            pl.BlockSpec((tm, ATTN_WIDTH), row),
            pl.BlockSpec((tm, GMLP_WIDTH), row),
            pl.BlockSpec((tm, D_MODEL), row),
            pl.BlockSpec((tm, PLE_DIM), row),
            pl.BlockSpec((D_MODEL, D_MODEL), const2, **resident),
            pl.BlockSpec((1, D_MODEL), const2),
            pl.BlockSpec((PLE_DIM, D_MODEL), const2, **resident),
            pl.BlockSpec((D_MODEL, D_MODEL), const2, **resident),
        ],
        out_specs=pl.BlockSpec((tm, D_MODEL), row),
        out_shape=jax.ShapeDtypeStruct((m, D_MODEL), _F32),
        compiler_params=pltpu.CompilerParams(
            dimension_semantics=("arbitrary",),
            vmem_limit_bytes=VMEM_LIMIT_BYTES),
        name="out_proj",
    )(a, m_gated, x2d, p2d, w_out_bf, post_g, w_pe_bf, w_pg_bf)


def _rope_tables(max_seq, tm):
    inv = 1.0 / (ROPE_THETA ** (jnp.arange(0, HEAD_DIM, 2, dtype=_F32) / HEAD_DIM))
    inv2 = jnp.concatenate([inv, inv])[None, :]
    ang_row = jnp.arange(tm, dtype=_F32)[:, None] * inv2
    ang_base = jnp.arange(0, max_seq, tm, dtype=_F32)[:, None] * inv2
    nb = max_seq // tm
    sign = jnp.broadcast_to(
        jnp.where(jnp.arange(HEAD_DIM) < HEAD_DIM // 2, -1.0, 1.0).astype(_F32)[None, :],
        (nb, HEAD_DIM))
    rope_row = jnp.stack([jnp.cos(ang_row), jnp.sin(ang_row)])
    rope_base = jnp.stack([jnp.cos(ang_base), jnp.sin(ang_base), sign]
                          + [jnp.zeros((nb, HEAD_DIM), _F32)] * 5, axis=1)
    return rope_row, rope_base


def _layer(x, p, params, tables, out_weights_f32=None, out_weights_bf=None):
    (pre_g, w_in_bf, sink, ln_g, ln_b, ws_bf, bs_b, post_g) = params
    batch, seq, _ = x.shape
    rope_row, rope_base = tables
    x2d = x.reshape(batch * seq, D_MODEL)
    p2d = p.reshape(batch * seq, PLE_DIM)
    q, k, v, ga, m_gated = _in_proj(x2d, seq, pre_g, w_in_bf, rope_row, rope_base,
                                    ln_g, ln_b, ws_bf, bs_b)
    if out_weights_bf is None:
        a, *out_weights_bf = _attn(q, k, v, ga, sink, batch, seq, cast_weights=out_weights_f32)
    else:
        a, = _attn(q, k, v, ga, sink, batch, seq)
    w_out_bf, w_pg_bf, w_pe_bf = out_weights_bf
    out = _out_proj(a, m_gated, x2d, p2d, w_out_bf, post_g, w_pe_bf, w_pg_bf)
    return out.reshape(batch, seq, D_MODEL), out_weights_bf


def kernel(x_prompt, x_sample, p_prompt, p_sample, pre_norm_g, w_in, attn_sink,
           gmlp_ln_g, gmlp_ln_b, gmlp_ws, gmlp_bs, w_out, post_norm_g, w_pe, w_pg):
    depth = w_in.shape[0]
    tables = _rope_tables(max(x_prompt.shape[1], x_sample.shape[1]), TM_IN)
    y_prompt, y_sample = x_prompt, x_sample
    for i in range(depth):
        params = (
            pre_norm_g[i].reshape(1, D_MODEL),
            w_in[i].astype(_BF16),
            attn_sink[i],
            gmlp_ln_g[i].reshape(1, GMLP_WIDTH),
            gmlp_ln_b[i].reshape(1, GMLP_WIDTH),
            gmlp_ws[i].astype(_BF16),
            jnp.broadcast_to(gmlp_bs[i][:, :, None], (N_GMLP_HEADS, CHUNK, HEAD_DIM)),
            post_norm_g[i].reshape(1, D_MODEL),
        )
        y_prompt, out_weights_bf = _layer(y_prompt, p_prompt[i], params, tables,
                                          out_weights_f32=(w_out[i], w_pg[i], w_pe[i]))
        y_sample, _ = _layer(y_sample, p_sample[i], params, tables,
                             out_weights_bf=out_weights_bf)
    return (y_prompt, y_sample)
```

```python
import functools
import math

import jax
import jax.numpy as jnp
from jax import lax
from jax.experimental import pallas as pl
from jax.experimental.pallas import tpu as pltpu

D_MODEL = 2048
HEAD_DIM = 128
ATTN_WIDTH = 1024
N_HEADS = 8
N_KV = 2
GROUP = 4
KV_WIDTH = 256
WINDOW = 128
BLOCK = 128
ROPE_THETA = 10000.0
GMLP_WIDTH = 1024
N_GMLP_HEADS = 8
CHUNK = 128
PLE_DIM = 256
EPS = 1e-6
IN_WIDTH = 5632
NEG_INF = -1e30
LOG2E = math.log2(math.e)
Q_SCALE = HEAD_DIM ** -0.5 * LOG2E

VMEM_LIMIT_BYTES = 56 * 1024 * 1024

TM_IN = 512
SUB_IN = 256
NCHUNK = 1024
TQ = 2048
TM_OUT = 512
SUB_OUT = 256

_BF16 = jnp.bfloat16
_F32 = jnp.float32


def _sigmoid(x):
    return 1.0 / (1.0 + jnp.exp(-x))


def _gelu_exact(x):
    return 0.5 * x * (1.0 + lax.erf(x * (2.0 ** -0.5)))


def _in_proj_kernel(x_ref, w_ref, rope_row_ref, rope_base_ref, ws_ref, bsb_ref,
                    g_ref, lng_ref, lnb_ref,
                    q_ref, k_ref, v_ref, ga_ref, m_ref, h_ref, vgf_ref, vn_ref):
    tm = x_ref.shape[0]
    rows = 64
    for r in range(tm // rows):
        x = x_ref[r * rows:(r + 1) * rows, :]
        ms = jnp.mean(x * x, axis=-1, keepdims=True)
        h = x * lax.rsqrt(ms + EPS) * g_ref[...]
        h_ref[r * rows:(r + 1) * rows, :] = h.astype(_BF16)

    cos_r, sin_r = rope_row_ref[0], rope_row_ref[1]
    base = rope_base_ref[pl.program_id(0) % rope_base_ref.shape[0]]
    cos_b, sin_b, sign = base[0:1, :], base[1:2, :], base[2:3, :]
    cos = cos_r * cos_b - sin_r * sin_b
    sin = (sin_r * cos_b + cos_r * sin_b) * sign

    def rope(t):
        return t * cos + pltpu.roll(t, HEAD_DIM // 2, 1) * sin

    def proj(start, width):
        return jnp.concatenate(
            [jnp.dot(h_ref[r * SUB_IN:(r + 1) * SUB_IN, :], w_ref[:, start:start + width],
                     preferred_element_type=_F32) for r in range(tm // SUB_IN)], axis=0)

    q0, k0, ga0 = 0, ATTN_WIDTH, ATTN_WIDTH + 2 * KV_WIDTH
    u0 = ga0 + ATTN_WIDTH
    vg0 = u0 + GMLP_WIDTH
    gg0 = vg0 + GMLP_WIDTH
    nsplit = ATTN_WIDTH // NCHUNK
    heads_per_chunk = NCHUNK // HEAD_DIM
    nchunks = tm // CHUNK
    mix_ref = vgf_ref

    def gmlp_v():
        for c in range(nsplit):
            ccols = slice(c * NCHUNK, (c + 1) * NCHUNK)
            vgf_ref[:, ccols] = _gelu_exact(proj(vg0 + c * NCHUNK, NCHUNK))

    def layer_norm():
        for r in range(tm // rows):
            rs = slice(r * rows, (r + 1) * rows)
            vf = vgf_ref[rs, :]
            mu = jnp.mean(vf, axis=-1, keepdims=True)
            vc = vf - mu
            var = jnp.mean(vc * vc, axis=-1, keepdims=True)
            vn = vc * lax.rsqrt(var + EPS) * lng_ref[...] + lnb_ref[...]
            vn_ref[rs, :] = vn.astype(_BF16)

    def spatial_mix():
        for h in range(N_GMLP_HEADS):
            cols = slice(h * HEAD_DIM, (h + 1) * HEAD_DIM)
            rhs = jnp.concatenate(
                [vn_ref[rc * CHUNK:(rc + 1) * CHUNK, cols] for rc in range(nchunks)], axis=1)
            mixed = jnp.dot(ws_ref[h], rhs, preferred_element_type=_F32)
            for rc in range(nchunks):
                mix_ref[rc * CHUNK:(rc + 1) * CHUNK, cols] = (
                    mixed[:, rc * HEAD_DIM:(rc + 1) * HEAD_DIM] + bsb_ref[h])

    def gmlp_gate():
        for c in range(nsplit):
            ccols = slice(c * NCHUNK, (c + 1) * NCHUNK)
            gu = _gelu_exact(proj(u0 + c * NCHUNK, NCHUNK))
            acc = proj(gg0 + c * NCHUNK, NCHUNK)
            m_ref[:, ccols] = (gu * mix_ref[:, ccols] * (acc * _sigmoid(acc))).astype(_BF16)

    def attn_q():
        for c in range(nsplit):
            acc = proj(q0 + c * NCHUNK, NCHUNK)
            for hd in range(heads_per_chunk):
                col = c * NCHUNK + hd * HEAD_DIM
                q_ref[:, col:col + HEAD_DIM] = (rope(
                    acc[:, hd * HEAD_DIM:(hd + 1) * HEAD_DIM]) * Q_SCALE).astype(_BF16)

    def attn_gate():
        for c in range(nsplit):
            acc = proj(ga0 + c * NCHUNK, NCHUNK)
            ga_ref[:, c * NCHUNK:(c + 1) * NCHUNK] = (acc * _sigmoid(acc)).astype(_BF16)

    def attn_kv():
        acc = proj(k0, KV_WIDTH)
        for hd in range(N_KV):
            k_ref[:, hd * HEAD_DIM:(hd + 1) * HEAD_DIM] = rope(
                acc[:, hd * HEAD_DIM:(hd + 1) * HEAD_DIM]).astype(_BF16)
        v_ref[...] = proj(k0 + KV_WIDTH, KV_WIDTH).astype(_BF16)

    gmlp_v()
    attn_q()
    layer_norm()
    spatial_mix()
    gmlp_gate()
    attn_gate()
    attn_kv()


def _in_proj(x2d, seq, pre_g, w_in_bf, rope_row, rope_base, ln_g, ln_b, ws_bf, bs_b):
    m = x2d.shape[0]
    tm = TM_IN
    assert m % tm == 0 and seq % tm == 0 and tm % CHUNK == 0
    blocks_per_seq = seq // tm
    const = lambda i: (0, 0)
    const3 = lambda i: (0, 0, 0)
    row = lambda i: (i, 0)
    resident = dict(pipeline_mode=pl.Buffered(1))
    out_widths = (ATTN_WIDTH, KV_WIDTH, KV_WIDTH, ATTN_WIDTH, GMLP_WIDTH)
    return pl.pallas_call(
        _in_proj_kernel,
        grid=(m // tm,),
        in_specs=[
            pl.BlockSpec((tm, D_MODEL), row),
            pl.BlockSpec((D_MODEL, IN_WIDTH), const, **resident),
            pl.BlockSpec((2, tm, HEAD_DIM), const3),
            pl.BlockSpec((blocks_per_seq, 8, HEAD_DIM), const3),
            pl.BlockSpec((N_GMLP_HEADS, CHUNK, CHUNK), const3, **resident),
            pl.BlockSpec((N_GMLP_HEADS, CHUNK, HEAD_DIM), const3, **resident),
            pl.BlockSpec((1, D_MODEL), const),
            pl.BlockSpec((1, GMLP_WIDTH), const),
            pl.BlockSpec((1, GMLP_WIDTH), const),
        ],
        out_specs=[pl.BlockSpec((tm, w), row) for w in out_widths],
        out_shape=[jax.ShapeDtypeStruct((m, w), _BF16) for w in out_widths],
        scratch_shapes=[pltpu.VMEM((tm, D_MODEL), _BF16),
                        pltpu.VMEM((tm, GMLP_WIDTH), _F32),
                        pltpu.VMEM((tm, GMLP_WIDTH), _BF16)],
        compiler_params=pltpu.CompilerParams(
            dimension_semantics=("arbitrary",),
            vmem_limit_bytes=VMEM_LIMIT_BYTES),
        name="in_proj",
    )(x2d, w_in_bf, rope_row, rope_base, ws_bf, bs_b, pre_g, ln_g, ln_b)


def _attn_kernel(seq, ncast, sink_ref, q_ref, kp_ref, kc_ref, kn_ref,
                 vp_ref, vc_ref, vn_ref, ga_ref, *rest):
    o_ref = rest[ncast]
    for src_ref, dst_ref in zip(rest[:ncast], rest[ncast + 1:]):
        dst_ref[...] = src_ref[...].astype(_BF16)

    n = pl.program_id(1)
    tq = q_ref.shape[0]
    nsub = tq // BLOCK

    ik = lax.broadcasted_iota(jnp.int32, (3 * BLOCK, BLOCK), 0)
    iq = lax.broadcasted_iota(jnp.int32, (3 * BLOCK, BLOCK), 1)
    band_t = jnp.abs(iq + BLOCK - ik) <= WINDOW
    eye = (lax.broadcasted_iota(jnp.int32, (BLOCK, BLOCK), 0)
           == lax.broadcasted_iota(jnp.int32, (BLOCK, BLOCK), 1)).astype(_BF16)
    onehot = jnp.concatenate([eye] * GROUP, axis=0)

    for j in range(nsub):
        kbase = n * tq + (j - 1) * BLOCK
        in_seq = (ik + kbase >= 0) & (ik + kbase < seq)
        bias_t = jnp.where(band_t & in_seq, 0.0, NEG_INF).astype(_BF16)
        for g in range(N_KV):
            lanes = slice(g * HEAD_DIM, (g + 1) * HEAD_DIM)

            def kv_rows(prev_ref, cur_ref, next_ref):
                lo = prev_ref[:, lanes] if j == 0 else cur_ref[(j - 1) * BLOCK:j * BLOCK, lanes]
                mid = cur_ref[j * BLOCK:(j + 1) * BLOCK, lanes]
                hi = (next_ref[:, lanes] if j == nsub - 1
                      else cur_ref[(j + 1) * BLOCK:(j + 2) * BLOCK, lanes])
                return jnp.concatenate([lo, mid, hi], axis=0)

            kk = jnp.concatenate([kv_rows(kp_ref, kc_ref, kn_ref), bias_t], axis=1)
            vv = jnp.concatenate(
                [kv_rows(vp_ref, vc_ref, vn_ref),
                 jnp.ones((3 * BLOCK, HEAD_DIM), _BF16)], axis=1)
            qs = jnp.concatenate(
                [q_ref[j * BLOCK:(j + 1) * BLOCK,
                       (g * GROUP + r) * HEAD_DIM:(g * GROUP + r + 1) * HEAD_DIM]
                 for r in range(GROUP)], axis=0)
            s = lax.dot_general(jnp.concatenate([qs, onehot], axis=1), kk,
                                (((1,), (1,)), ((), ())),
                                preferred_element_type=_F32)
            sink = jnp.concatenate(
                [jnp.full((BLOCK, HEAD_DIM), sink_ref[g * GROUP + r] * LOG2E, _F32)
                 for r in range(GROUP)], axis=0)
            sb = [s[:, i * BLOCK:(i + 1) * BLOCK] for i in range(3)]
            rowmax = jnp.max(jnp.maximum(jnp.maximum(sb[0], sb[1]), sb[2]),
                             axis=-1, keepdims=True)
            mx = jnp.maximum(jnp.broadcast_to(rowmax, (GROUP * BLOCK, HEAD_DIM)), sink)
            p = jnp.concatenate([jnp.exp2(t - mx) for t in sb], axis=1).astype(_BF16)
            o = jnp.dot(p, vv, preferred_element_type=_F32)
            denom = o[:, HEAD_DIM:] + jnp.exp2(sink - mx)
            o = o[:, :HEAD_DIM] * (1.0 / denom)
            for r in range(GROUP):
                cols = slice((g * GROUP + r) * HEAD_DIM, (g * GROUP + r + 1) * HEAD_DIM)
                gate = ga_ref[j * BLOCK:(j + 1) * BLOCK, cols].astype(_F32)
                o_ref[j * BLOCK:(j + 1) * BLOCK, cols] = (
                    o[r * BLOCK:(r + 1) * BLOCK, :] * gate).astype(_BF16)


def _attn(q, k, v, ga, sink, batch, seq, cast_weights=()):
    tq = TQ
    assert seq % tq == 0
    sub = tq // BLOCK
    nblk = seq // BLOCK
    nq = seq // tq
    nsteps = batch * nq
    slab = lambda b, n: (b * nq + n, 0)
    cast_specs = []
    for w in cast_weights:
        rows = w.shape[0] // nsteps
        assert w.shape[0] % nsteps == 0 and rows % 16 == 0
        cast_specs.append(pl.BlockSpec((rows, w.shape[1]), slab))
    q3 = q.reshape(batch, seq, ATTN_WIDTH)
    k3 = k.reshape(batch, seq, KV_WIDTH)
    v3 = v.reshape(batch, seq, KV_WIDTH)
    ga3 = ga.reshape(batch, seq, ATTN_WIDTH)
    cur = lambda b, n: (b, n, 0)
    prev = lambda b, n: (b, jnp.maximum(n * sub - 1, 0), 0)
    nxt = lambda b, n: (b, jnp.minimum(n * sub + sub, nblk - 1), 0)
    kv_specs = [
        pl.BlockSpec((None, BLOCK, KV_WIDTH), prev),
        pl.BlockSpec((None, tq, KV_WIDTH), cur),
        pl.BlockSpec((None, BLOCK, KV_WIDTH), nxt),
    ]
    outs = pl.pallas_call(
        functools.partial(_attn_kernel, seq, len(cast_weights)),
        grid=(batch, nq),
        in_specs=[pl.BlockSpec(memory_space=pltpu.SMEM),
                  pl.BlockSpec((None, tq, ATTN_WIDTH), cur)]
                 + kv_specs + kv_specs
                 + [pl.BlockSpec((None, tq, ATTN_WIDTH), cur)]
                 + cast_specs,
        out_specs=[pl.BlockSpec((None, tq, ATTN_WIDTH), cur)] + cast_specs,
        out_shape=[jax.ShapeDtypeStruct((batch, seq, ATTN_WIDTH), _BF16)]
                  + [jax.ShapeDtypeStruct(w.shape, _BF16) for w in cast_weights],
        compiler_params=pltpu.CompilerParams(
            dimension_semantics=("arbitrary", "arbitrary"),
            vmem_limit_bytes=VMEM_LIMIT_BYTES),
        name="window_attn",
    )(sink, q3, k3, k3, k3, v3, v3, v3, ga3, *cast_weights)
    return (outs[0].reshape(batch * seq, ATTN_WIDTH), *outs[1:])


def _out_proj_kernel(a_ref, m_ref, x_ref, p_ref, wout_ref, wpe_ref, wpg_ref, postg_ref, o_ref):
    tm = x_ref.shape[0]
    subs = [slice(t * SUB_OUT, (t + 1) * SUB_OUT) for t in range(tm // SUB_OUT)]

    def mix_proj(sub):
        am = jnp.concatenate([a_ref[sub, :], m_ref[sub, :]], axis=1)
        return jnp.dot(am, wout_ref[...], preferred_element_type=_F32)

    def post_norm(sub, y):
        ms = jnp.mean(y * y, axis=-1, keepdims=True)
        return x_ref[sub, :] + y * lax.rsqrt(ms + EPS) * postg_ref[0:1, :]

    def ple(sub, x1):
        gate = _sigmoid(jnp.dot(x1.astype(_BF16), wpg_ref[...], preferred_element_type=_F32))
        pe = jnp.dot(p_ref[sub, :].astype(_BF16), wpe_ref[...], preferred_element_type=_F32)
        o_ref[sub, :] = x1 + gate * pe

    ys = [mix_proj(sub) for sub in subs]
    for sub, y in zip(subs, ys):
        ple(sub, post_norm(sub, y))


def _out_proj(a, m_gated, x2d, p2d, w_out_bf, post_g, w_pe_bf, w_pg_bf):
    m = x2d.shape[0]
    tm = TM_OUT
    assert m % tm == 0 and tm % SUB_OUT == 0
    row = lambda i: (i, 0)
    const2 = lambda i: (0, 0)
    resident = dict(pipeline_mode=pl.Buffered(1))
    return pl.pallas_call(
        _out_proj_kernel,
        grid=(m // tm,),
        in_specs=[
            pl.BlockSpec((tm, ATTN_WIDTH), row),
            pl.BlockSpec((tm, GMLP_WIDTH), row),
            pl.BlockSpec((tm, D_MODEL), row),
            pl.BlockSpec((tm, PLE_DIM), row),
            pl.BlockSpec((D_MODEL, D_MODEL), const2, **resident),
            pl.BlockSpec((PLE_DIM, D_MODEL), const2, **resident),
            pl.BlockSpec((D_MODEL, D_MODEL), const2, **resident),
            pl.BlockSpec((8, D_MODEL), const2),
        ],
        out_specs=pl.BlockSpec((tm, D_MODEL), row),
        out_shape=jax.ShapeDtypeStruct((m, D_MODEL), _F32),
        compiler_params=pltpu.CompilerParams(
            dimension_semantics=("arbitrary",),
            vmem_limit_bytes=VMEM_LIMIT_BYTES),
        name="out_proj",
    )(a, m_gated, x2d, p2d, w_out_bf, w_pe_bf, w_pg_bf, post_g)


def _rope_tables(max_seq, tm):
    inv = 1.0 / (ROPE_THETA ** (jnp.arange(0, HEAD_DIM, 2, dtype=_F32) / HEAD_DIM))
    inv2 = jnp.concatenate([inv, inv])[None, :]
    ang_row = jnp.arange(tm, dtype=_F32)[:, None] * inv2
    ang_base = jnp.arange(0, max_seq, tm, dtype=_F32)[:, None] * inv2
    nb = max_seq // tm
    sign = jnp.broadcast_to(
        jnp.where(jnp.arange(HEAD_DIM) < HEAD_DIM // 2, -1.0, 1.0).astype(_F32)[None, :],
        (nb, HEAD_DIM))
    rope_row = jnp.stack([jnp.cos(ang_row), jnp.sin(ang_row)])
    rope_base = jnp.stack([jnp.cos(ang_base), jnp.sin(ang_base), sign]
                          + [jnp.zeros((nb, HEAD_DIM), _F32)] * 5, axis=1)
    return rope_row, rope_base


def _layer(x, p, params, tables, out_weights_f32=None, out_weights_bf=None):
    (pre_g, w_in_bf, sink, ln_g, ln_b, ws_bf, bs_b, post_g) = params
    batch, seq, _ = x.shape
    rope_row, rope_base = tables
    x2d = x.reshape(batch * seq, D_MODEL)
    p2d = p.reshape(batch * seq, PLE_DIM)
    q, k, v, ga, m_gated = _in_proj(x2d, seq, pre_g, w_in_bf, rope_row, rope_base,
                                    ln_g, ln_b, ws_bf, bs_b)
    if out_weights_bf is None:
        a, *out_weights_bf = _attn(q, k, v, ga, sink, batch, seq, cast_weights=out_weights_f32)
    else:
        a, = _attn(q, k, v, ga, sink, batch, seq)
    w_out_bf, w_pg_bf, w_pe_bf = out_weights_bf
    out = _out_proj(a, m_gated, x2d, p2d, w_out_bf, post_g, w_pe_bf, w_pg_bf)
    return out.reshape(batch, seq, D_MODEL), out_weights_bf


def kernel(x_prompt, x_sample, p_prompt, p_sample, pre_norm_g, w_in, attn_sink,
           gmlp_ln_g, gmlp_ln_b, gmlp_ws, gmlp_bs, w_out, post_norm_g, w_pe, w_pg):
    depth = w_in.shape[0]
    tables = _rope_tables(max(x_prompt.shape[1], x_sample.shape[1]), TM_IN)
    y_prompt, y_sample = x_prompt, x_sample
    for i in range(depth):
        params = (
            pre_norm_g[i].reshape(1, D_MODEL),
            w_in[i].astype(_BF16),
            attn_sink[i],
            gmlp_ln_g[i].reshape(1, GMLP_WIDTH),
            gmlp_ln_b[i].reshape(1, GMLP_WIDTH),
            gmlp_ws[i].astype(_BF16),
            jnp.broadcast_to(gmlp_bs[i][:, :, None], (N_GMLP_HEADS, CHUNK, HEAD_DIM)),
            jnp.broadcast_to(post_norm_g[i].reshape(1, D_MODEL), (8, D_MODEL)),
        )
        y_prompt, out_weights_bf = _layer(y_prompt, p_prompt[i], params, tables,
                                          out_weights_f32=(w_out[i], w_pg[i], w_pe[i]))
        y_sample, _ = _layer(y_sample, p_sample[i], params, tables,
                             out_weights_bf=out_weights_bf)
    return (y_prompt, y_sample)
```

```python
import functools
import math

import jax
import jax.numpy as jnp
from jax import lax
from jax.experimental import pallas as pl
from jax.experimental.pallas import tpu as pltpu

D_MODEL = 2048
HEAD_DIM = 128
ATTN_WIDTH = 1024
N_HEADS = 8
N_KV = 2
GROUP = 4
KV_WIDTH = 256
WINDOW = 128
BLOCK = 128
ROPE_THETA = 10000.0
GMLP_WIDTH = 1024
N_GMLP_HEADS = 8
CHUNK = 128
PLE_DIM = 256
EPS = 1e-6
IN_WIDTH = 5632
NEG_INF = -1e30
LOG2E = math.log2(math.e)
Q_SCALE = HEAD_DIM ** -0.5 * LOG2E

VMEM_LIMIT_BYTES = 56 * 1024 * 1024

TM_IN = 512
SUB_IN = 256
NCHUNK = 512
TQ = 2048
TM_OUT = 512
SUB_OUT = 256
PLE_COLUMN_PIECES = ((0, 1024), (1024, 1792), (1792, 2048))

_BF16 = jnp.bfloat16
_F32 = jnp.float32


def _sigmoid(x):
    return 1.0 / (1.0 + jnp.exp(-x))


def _gelu_exact(x):
    return 0.5 * x * (1.0 + lax.erf(x * (2.0 ** -0.5)))


def _in_proj_kernel(x_ref, g_ref, w_ref, rope_row_ref, rope_base_ref, lng_ref, lnb_ref, ws_ref,
                    bsb_ref,
                    q_ref, k_ref, v_ref, ga_ref, m_ref, h_ref, vgf_ref, vn_ref):
    tm = x_ref.shape[0]
    rows = 64
    for r in range(tm // rows):
        x = x_ref[r * rows:(r + 1) * rows, :]
        ms = jnp.mean(x * x, axis=-1, keepdims=True)
        h = x * lax.rsqrt(ms + EPS) * g_ref[...]
        h_ref[r * rows:(r + 1) * rows, :] = h.astype(_BF16)

    cos_r, sin_r = rope_row_ref[0], rope_row_ref[1]
    base = rope_base_ref[pl.program_id(0) % rope_base_ref.shape[0]]
    cos_b, sin_b, sign = base[0:1, :], base[1:2, :], base[2:3, :]
    cos = cos_r * cos_b - sin_r * sin_b
    sin = (sin_r * cos_b + cos_r * sin_b) * sign

    def rope(t):
        return t * cos + pltpu.roll(t, HEAD_DIM // 2, 1) * sin

    def proj(start, width):
        return jnp.concatenate(
            [jnp.dot(h_ref[r * SUB_IN:(r + 1) * SUB_IN, :], w_ref[:, start:start + width],
                     preferred_element_type=_F32) for r in range(tm // SUB_IN)], axis=0)

    q0, k0, ga0 = 0, ATTN_WIDTH, ATTN_WIDTH + 2 * KV_WIDTH
    u0 = ga0 + ATTN_WIDTH
    vg0 = u0 + GMLP_WIDTH
    gg0 = vg0 + GMLP_WIDTH
    nsplit = ATTN_WIDTH // NCHUNK
    heads_per_chunk = NCHUNK // HEAD_DIM
    nchunks = tm // CHUNK
    mix_ref = vgf_ref

    def gmlp_v():
        for c in range(nsplit):
            ccols = slice(c * NCHUNK, (c + 1) * NCHUNK)
            vgf_ref[:, ccols] = _gelu_exact(proj(vg0 + c * NCHUNK, NCHUNK))

    def layer_norm():
        for r in range(tm // rows):
            rs = slice(r * rows, (r + 1) * rows)
            vf = vgf_ref[rs, :]
            mu = jnp.mean(vf, axis=-1, keepdims=True)
            vc = vf - mu
            var = jnp.mean(vc * vc, axis=-1, keepdims=True)
            vn = vc * lax.rsqrt(var + EPS) * lng_ref[...] + lnb_ref[...]
            vn_ref[rs, :] = vn.astype(_BF16)

    def spatial_mix():
        for h in range(N_GMLP_HEADS):
            cols = slice(h * HEAD_DIM, (h + 1) * HEAD_DIM)
            rhs = jnp.concatenate(
                [vn_ref[rc * CHUNK:(rc + 1) * CHUNK, cols] for rc in range(nchunks)], axis=1)
            mixed = jnp.dot(ws_ref[h], rhs, preferred_element_type=_F32)
            for rc in range(nchunks):
                mix_ref[rc * CHUNK:(rc + 1) * CHUNK, cols] = (
                    mixed[:, rc * HEAD_DIM:(rc + 1) * HEAD_DIM] + bsb_ref[h])

    def gmlp_gate():
        for c in range(nsplit):
            ccols = slice(c * NCHUNK, (c + 1) * NCHUNK)
            gu = _gelu_exact(proj(u0 + c * NCHUNK, NCHUNK))
            acc = proj(gg0 + c * NCHUNK, NCHUNK)
            m_ref[:, ccols] = (gu * mix_ref[:, ccols] * (acc * _sigmoid(acc))).astype(_BF16)

    def attn_q():
        for c in range(nsplit):
            acc = proj(q0 + c * NCHUNK, NCHUNK)
            for hd in range(heads_per_chunk):
                col = c * NCHUNK + hd * HEAD_DIM
                q_ref[:, col:col + HEAD_DIM] = (rope(
                    acc[:, hd * HEAD_DIM:(hd + 1) * HEAD_DIM]) * Q_SCALE).astype(_BF16)

    def attn_gate():
        for c in range(nsplit):
            acc = proj(ga0 + c * NCHUNK, NCHUNK)
            ga_ref[:, c * NCHUNK:(c + 1) * NCHUNK] = (acc * _sigmoid(acc)).astype(_BF16)

    def attn_kv():
        acc = proj(k0, KV_WIDTH)
        for hd in range(N_KV):
            k_ref[:, hd * HEAD_DIM:(hd + 1) * HEAD_DIM] = rope(
                acc[:, hd * HEAD_DIM:(hd + 1) * HEAD_DIM]).astype(_BF16)
        v_ref[...] = proj(k0 + KV_WIDTH, KV_WIDTH).astype(_BF16)

    gmlp_v()
    attn_q()
    layer_norm()
    spatial_mix()
    gmlp_gate()
    attn_gate()
    attn_kv()


def _in_proj(x2d, seq, pre_g, w_in_bf, rope_row, rope_base, ln_g, ln_b, ws_bf, bs_b):
    m = x2d.shape[0]
    tm = TM_IN
    assert m % tm == 0 and seq % tm == 0 and tm % CHUNK == 0
    blocks_per_seq = seq // tm
    const = lambda i: (0, 0)
    const3 = lambda i: (0, 0, 0)
    row = lambda i: (i, 0)
    resident = dict(pipeline_mode=pl.Buffered(1))
    out_widths = (ATTN_WIDTH, KV_WIDTH, KV_WIDTH, ATTN_WIDTH, GMLP_WIDTH)
    return pl.pallas_call(
        _in_proj_kernel,
        grid=(m // tm,),
        in_specs=[
            pl.BlockSpec((tm, D_MODEL), row),
            pl.BlockSpec((1, D_MODEL), const),
            pl.BlockSpec((D_MODEL, IN_WIDTH), const, **resident),
            pl.BlockSpec((2, tm, HEAD_DIM), const3),
            pl.BlockSpec((blocks_per_seq, 8, HEAD_DIM), const3),
            pl.BlockSpec((1, GMLP_WIDTH), const),
            pl.BlockSpec((1, GMLP_WIDTH), const),
            pl.BlockSpec((N_GMLP_HEADS, CHUNK, CHUNK), const3, **resident),
            pl.BlockSpec((N_GMLP_HEADS, CHUNK, HEAD_DIM), const3, **resident),
        ],
        out_specs=[pl.BlockSpec((tm, w), row) for w in out_widths],
        out_shape=[jax.ShapeDtypeStruct((m, w), _BF16) for w in out_widths],
        scratch_shapes=[pltpu.VMEM((tm, D_MODEL), _BF16),
                        pltpu.VMEM((tm, GMLP_WIDTH), _F32),
                        pltpu.VMEM((tm, GMLP_WIDTH), _BF16)],
        compiler_params=pltpu.CompilerParams(
            dimension_semantics=("arbitrary",),
            vmem_limit_bytes=VMEM_LIMIT_BYTES),
        name="in_proj",
    )(x2d, pre_g, w_in_bf, rope_row, rope_base, ln_g, ln_b, ws_bf, bs_b)


def _attn_kernel(seq, ncast, sink_ref, q_ref, kp_ref, kc_ref, kn_ref,
                 vp_ref, vc_ref, vn_ref, ga_ref, *rest):
    o_ref = rest[ncast]
    for src_ref, dst_ref in zip(rest[:ncast], rest[ncast + 1:]):
        dst_ref[...] = src_ref[...].astype(_BF16)

    n = pl.program_id(1)
    tq = q_ref.shape[0]
    nsub = tq // BLOCK

    ik = lax.broadcasted_iota(jnp.int32, (3 * BLOCK, BLOCK), 0)
    iq = lax.broadcasted_iota(jnp.int32, (3 * BLOCK, BLOCK), 1)
    band_t = jnp.abs(iq + BLOCK - ik) <= WINDOW
    eye = (lax.broadcasted_iota(jnp.int32, (BLOCK, BLOCK), 0)
           == lax.broadcasted_iota(jnp.int32, (BLOCK, BLOCK), 1)).astype(_BF16)
    onehot = jnp.concatenate([eye] * GROUP, axis=0)

    for j in range(nsub):
        kbase = n * tq + (j - 1) * BLOCK
        in_seq = (ik + kbase >= 0) & (ik + kbase < seq)
        bias_t = jnp.where(band_t & in_seq, 0.0, NEG_INF).astype(_BF16)
        for g in range(N_KV):
            lanes = slice(g * HEAD_DIM, (g + 1) * HEAD_DIM)

            def kv_rows(prev_ref, cur_ref, next_ref):
                lo = prev_ref[:, lanes] if j == 0 else cur_ref[(j - 1) * BLOCK:j * BLOCK, lanes]
                mid = cur_ref[j * BLOCK:(j + 1) * BLOCK, lanes]
                hi = (next_ref[:, lanes] if j == nsub - 1
                      else cur_ref[(j + 1) * BLOCK:(j + 2) * BLOCK, lanes])
                return jnp.concatenate([lo, mid, hi], axis=0)

            kk = jnp.concatenate([kv_rows(kp_ref, kc_ref, kn_ref), bias_t], axis=1)
            vv = jnp.concatenate(
                [kv_rows(vp_ref, vc_ref, vn_ref),
                 jnp.ones((3 * BLOCK, HEAD_DIM), _BF16)], axis=1)
            qs = jnp.concatenate(
                [q_ref[j * BLOCK:(j + 1) * BLOCK,
                       (g * GROUP + r) * HEAD_DIM:(g * GROUP + r + 1) * HEAD_DIM]
                 for r in range(GROUP)], axis=0)
            s = lax.dot_general(jnp.concatenate([qs, onehot], axis=1), kk,
                                (((1,), (1,)), ((), ())),
                                preferred_element_type=_F32)
            sink = jnp.concatenate(
                [jnp.full((BLOCK, HEAD_DIM), sink_ref[g * GROUP + r] * LOG2E, _F32)
                 for r in range(GROUP)], axis=0)
            sb = [s[:, i * BLOCK:(i + 1) * BLOCK] for i in range(3)]
            rowmax = jnp.max(jnp.maximum(jnp.maximum(sb[0], sb[1]), sb[2]),
                             axis=-1, keepdims=True)
            mx = jnp.maximum(jnp.broadcast_to(rowmax, (GROUP * BLOCK, HEAD_DIM)), sink)
            p = jnp.concatenate([jnp.exp2(t - mx) for t in sb], axis=1).astype(_BF16)
            o = jnp.dot(p, vv, preferred_element_type=_F32)
            denom = o[:, HEAD_DIM:] + jnp.exp2(sink - mx)
            o = o[:, :HEAD_DIM] * (1.0 / denom)
            for r in range(GROUP):
                cols = slice((g * GROUP + r) * HEAD_DIM, (g * GROUP + r + 1) * HEAD_DIM)
                gate = ga_ref[j * BLOCK:(j + 1) * BLOCK, cols].astype(_F32)
                o_ref[j * BLOCK:(j + 1) * BLOCK, cols] = (
                    o[r * BLOCK:(r + 1) * BLOCK, :] * gate).astype(_BF16)


def _attn(q, k, v, ga, sink, batch, seq, cast_weights=()):
    tq = TQ
    assert seq % tq == 0
    sub = tq // BLOCK
    nblk = seq // BLOCK
    nq = seq // tq
    nsteps = batch * nq
    slab = lambda b, n: (b * nq + n, 0)
    cast_specs = []
    for w in cast_weights:
        rows = w.shape[0] // nsteps
        assert w.shape[0] % nsteps == 0 and rows % 16 == 0
        cast_specs.append(pl.BlockSpec((rows, w.shape[1]), slab))
    q3 = q.reshape(batch, seq, ATTN_WIDTH)
    k3 = k.reshape(batch, seq, KV_WIDTH)
    v3 = v.reshape(batch, seq, KV_WIDTH)
    ga3 = ga.reshape(batch, seq, ATTN_WIDTH)
    cur = lambda b, n: (b, n, 0)
    prev = lambda b, n: (b, jnp.maximum(n * sub - 1, 0), 0)
    nxt = lambda b, n: (b, jnp.minimum(n * sub + sub, nblk - 1), 0)
    kv_specs = [
        pl.BlockSpec((None, BLOCK, KV_WIDTH), prev),
        pl.BlockSpec((None, tq, KV_WIDTH), cur),
        pl.BlockSpec((None, BLOCK, KV_WIDTH), nxt),
    ]
    outs = pl.pallas_call(
        functools.partial(_attn_kernel, seq, len(cast_weights)),
        grid=(batch, nq),
        in_specs=[pl.BlockSpec(memory_space=pltpu.SMEM),
                  pl.BlockSpec((None, tq, ATTN_WIDTH), cur)]
                 + kv_specs + kv_specs
                 + [pl.BlockSpec((None, tq, ATTN_WIDTH), cur)]
                 + cast_specs,
        out_specs=[pl.BlockSpec((None, tq, ATTN_WIDTH), cur)] + cast_specs,
        out_shape=[jax.ShapeDtypeStruct((batch, seq, ATTN_WIDTH), _BF16)]
                  + [jax.ShapeDtypeStruct(w.shape, _BF16) for w in cast_weights],
        compiler_params=pltpu.CompilerParams(
            dimension_semantics=("arbitrary", "arbitrary"),
            vmem_limit_bytes=VMEM_LIMIT_BYTES),
        name="window_attn",
    )(sink, q3, k3, k3, k3, v3, v3, v3, ga3, *cast_weights)
    return (outs[0].reshape(batch * seq, ATTN_WIDTH), *outs[1:])


def _out_proj_kernel(a_ref, m_ref, x_ref, p_ref, wout_ref, postg_ref, wpe_ref, wpg_ref, o_ref):
    tm = x_ref.shape[0]
    subs = [slice(t * SUB_OUT, (t + 1) * SUB_OUT) for t in range(tm // SUB_OUT)]

    def mix_proj(sub):
        am = jnp.concatenate([a_ref[sub, :], m_ref[sub, :]], axis=1)
        return jnp.dot(am, wout_ref[...], preferred_element_type=_F32)

    def post_norm(sub, y):
        ms = jnp.mean(y * y, axis=-1, keepdims=True)
        return x_ref[sub, :] + y * lax.rsqrt(ms + EPS) * postg_ref[...]

    def ple(sub, x1):
        x1b = x1.astype(_BF16)
        pb = p_ref[sub, :].astype(_BF16)
        for c0, c1 in PLE_COLUMN_PIECES:
            gate = _sigmoid(jnp.dot(x1b, wpg_ref[:, c0:c1], preferred_element_type=_F32))
            pe = jnp.dot(pb, wpe_ref[:, c0:c1], preferred_element_type=_F32)
            o_ref[sub, c0:c1] = x1[:, c0:c1] + gate * pe

    ys = [mix_proj(sub) for sub in subs]
    for sub, y in zip(subs, ys):
        ple(sub, post_norm(sub, y))


def _out_proj(a, m_gated, x2d, p2d, w_out_bf, post_g, w_pe_bf, w_pg_bf):
    m = x2d.shape[0]
    tm = TM_OUT
    assert m % tm == 0 and tm % SUB_OUT == 0
    row = lambda i: (i, 0)
    const2 = lambda i: (0, 0)
    resident = dict(pipeline_mode=pl.Buffered(1))
    return pl.pallas_call(
        _out_proj_kernel,
        grid=(m // tm,),
        in_specs=[
            pl.BlockSpec((tm, ATTN_WIDTH), row),
            pl.BlockSpec((tm, GMLP_WIDTH), row),
            pl.BlockSpec((tm, D_MODEL), row),
            pl.BlockSpec((tm, PLE_DIM), row),
            pl.BlockSpec((D_MODEL, D_MODEL), const2, **resident),
            pl.BlockSpec((1, D_MODEL), const2),
            pl.BlockSpec((PLE_DIM, D_MODEL), const2, **resident),
            pl.BlockSpec((D_MODEL, D_MODEL), const2, **resident),
        ],
        out_specs=pl.BlockSpec((tm, D_MODEL), row),
        out_shape=jax.ShapeDtypeStruct((m, D_MODEL), _F32),
        compiler_params=pltpu.CompilerParams(
            dimension_semantics=("arbitrary",),
            vmem_limit_bytes=VMEM_LIMIT_BYTES),
        name="out_proj",
    )(a, m_gated, x2d, p2d, w_out_bf, post_g, w_pe_bf, w_pg_bf)


def _rope_tables(max_seq, tm):
    inv = 1.0 / (ROPE_THETA ** (jnp.arange(0, HEAD_DIM, 2, dtype=_F32) / HEAD_DIM))
    inv2 = jnp.concatenate([inv, inv])[None, :]
    ang_row = jnp.arange(tm, dtype=_F32)[:, None] * inv2
    ang_base = jnp.arange(0, max_seq, tm, dtype=_F32)[:, None] * inv2
    nb = max_seq // tm
    sign = jnp.broadcast_to(
        jnp.where(jnp.arange(HEAD_DIM) < HEAD_DIM // 2, -1.0, 1.0).astype(_F32)[None, :],
        (nb, HEAD_DIM))
    rope_row = jnp.stack([jnp.cos(ang_row), jnp.sin(ang_row)])
    rope_base = jnp.stack([jnp.cos(ang_base), jnp.sin(ang_base), sign]
                          + [jnp.zeros((nb, HEAD_DIM), _F32)] * 5, axis=1)
    return rope_row, rope_base


def _layer(x, p, params, tables, out_weights_f32=None, out_weights_bf=None):
    (pre_g, w_in_bf, sink, ln_g, ln_b, ws_bf, bs_b, post_g) = params
    batch, seq, _ = x.shape
    rope_row, rope_base = tables
    x2d = x.reshape(batch * seq, D_MODEL)
    p2d = p.reshape(batch * seq, PLE_DIM)
    q, k, v, ga, m_gated = _in_proj(x2d, seq, pre_g, w_in_bf, rope_row, rope_base,
                                    ln_g, ln_b, ws_bf, bs_b)
    if out_weights_bf is None:
        a, *out_weights_bf = _attn(q, k, v, ga, sink, batch, seq, cast_weights=out_weights_f32)
    else:
        a, = _attn(q, k, v, ga, sink, batch, seq)
    w_out_bf, w_pg_bf, w_pe_bf = out_weights_bf
    out = _out_proj(a, m_gated, x2d, p2d, w_out_bf, post_g, w_pe_bf, w_pg_bf)
    return out.reshape(batch, seq, D_MODEL), out_weights_bf


def kernel(x_prompt, x_sample, p_prompt, p_sample, pre_norm_g, w_in, attn_sink,
           gmlp_ln_g, gmlp_ln_b, gmlp_ws, gmlp_bs, w_out, post_norm_g, w_pe, w_pg):
    depth = w_in.shape[0]
    tables = _rope_tables(max(x_prompt.shape[1], x_sample.shape[1]), TM_IN)
    y_prompt, y_sample = x_prompt, x_sample
    for i in range(depth):
        params = (
            pre_norm_g[i].reshape(1, D_MODEL),
            w_in[i].astype(_BF16),
            attn_sink[i],
            gmlp_ln_g[i].reshape(1, GMLP_WIDTH),
            gmlp_ln_b[i].reshape(1, GMLP_WIDTH),
            gmlp_ws[i].astype(_BF16),
            jnp.broadcast_to(gmlp_bs[i][:, :, None], (N_GMLP_HEADS, CHUNK, HEAD_DIM)),
            post_norm_g[i].reshape(1, D_MODEL),
        )
        y_prompt, out_weights_bf = _layer(y_prompt, p_prompt[i], params, tables,
                                          out_weights_f32=(w_out[i], w_pg[i], w_pe[i]))
        y_sample, _ = _layer(y_sample, p_sample[i], params, tables,
                             out_weights_bf=out_weights_bf)
    return (y_prompt, y_sample)
```

```python
import functools
import math

import jax
import jax.numpy as jnp
from jax import lax
from jax.experimental import pallas as pl
from jax.experimental.pallas import tpu as pltpu

D_MODEL = 2048
HEAD_DIM = 128
ATTN_WIDTH = 1024
N_HEADS = 8
N_KV = 2
GROUP = 4
KV_WIDTH = 256
WINDOW = 128
BLOCK = 128
ROPE_THETA = 10000.0
GMLP_WIDTH = 1024
N_GMLP_HEADS = 8
CHUNK = 128
PLE_DIM = 256
EPS = 1e-6
IN_WIDTH = 5632
NEG_INF = -1e30
LOG2E = math.log2(math.e)
Q_SCALE = HEAD_DIM ** -0.5 * LOG2E

VMEM_LIMIT_BYTES = 56 * 1024 * 1024

TM_IN = 512
SUB_IN = 128
NCHUNK = 512
TQ = 2048
TM_OUT = 512
SUB_OUT = 256
PLE_COLUMN_PIECES = ((0, 1024), (1024, 1792), (1792, 2048))

_BF16 = jnp.bfloat16
_F32 = jnp.float32


def _sigmoid(x):
    return 1.0 / (1.0 + jnp.exp(-x))


def _gelu_exact(x):
    return 0.5 * x * (1.0 + lax.erf(x * (2.0 ** -0.5)))


def _in_proj_kernel(x_ref, g_ref, w_ref, rope_row_ref, rope_base_ref, lng_ref, lnb_ref, ws_ref,
                    bsb_ref,
                    q_ref, k_ref, v_ref, ga_ref, m_ref, h_ref, vgf_ref, vn_ref):
    tm = x_ref.shape[0]
    rows = 64
    for r in range(tm // rows):
        x = x_ref[r * rows:(r + 1) * rows, :]
        ms = jnp.mean(x * x, axis=-1, keepdims=True)
        h = x * lax.rsqrt(ms + EPS) * g_ref[...]
        h_ref[r * rows:(r + 1) * rows, :] = h.astype(_BF16)

    cos_r, sin_r = rope_row_ref[0], rope_row_ref[1]
    base = rope_base_ref[pl.program_id(0) % rope_base_ref.shape[0]]
    cos_b, sin_b, sign = base[0:1, :], base[1:2, :], base[2:3, :]
    cos = cos_r * cos_b - sin_r * sin_b
    sin = (sin_r * cos_b + cos_r * sin_b) * sign

    def rope(t):
        return t * cos + pltpu.roll(t, HEAD_DIM // 2, 1) * sin

    def proj(start, width):
        return jnp.concatenate(
            [jnp.dot(h_ref[r * SUB_IN:(r + 1) * SUB_IN, :], w_ref[:, start:start + width],
                     preferred_element_type=_F32) for r in range(tm // SUB_IN)], axis=0)

    q0, k0, ga0 = 0, ATTN_WIDTH, ATTN_WIDTH + 2 * KV_WIDTH
    u0 = ga0 + ATTN_WIDTH
    vg0 = u0 + GMLP_WIDTH
    gg0 = vg0 + GMLP_WIDTH
    nsplit = ATTN_WIDTH // NCHUNK
    heads_per_chunk = NCHUNK // HEAD_DIM
    nchunks = tm // CHUNK
    mix_ref = vgf_ref

    def gmlp_v():
        for c in range(nsplit):
            ccols = slice(c * NCHUNK, (c + 1) * NCHUNK)
            vgf_ref[:, ccols] = _gelu_exact(proj(vg0 + c * NCHUNK, NCHUNK))

    def layer_norm():
        for r in range(tm // rows):
            rs = slice(r * rows, (r + 1) * rows)
            vf = vgf_ref[rs, :]
            mu = jnp.mean(vf, axis=-1, keepdims=True)
            vc = vf - mu
            var = jnp.mean(vc * vc, axis=-1, keepdims=True)
            vn = vc * lax.rsqrt(var + EPS) * lng_ref[...] + lnb_ref[...]
            vn_ref[rs, :] = vn.astype(_BF16)

    def spatial_mix():
        for h in range(N_GMLP_HEADS):
            cols = slice(h * HEAD_DIM, (h + 1) * HEAD_DIM)
            rhs = jnp.concatenate(
                [vn_ref[rc * CHUNK:(rc + 1) * CHUNK, cols] for rc in range(nchunks)], axis=1)
            mixed = jnp.dot(ws_ref[h], rhs, preferred_element_type=_F32)
            for rc in range(nchunks):
                mix_ref[rc * CHUNK:(rc + 1) * CHUNK, cols] = (
                    mixed[:, rc * HEAD_DIM:(rc + 1) * HEAD_DIM] + bsb_ref[h])

    def gmlp_gate():
        for c in range(nsplit):
            ccols = slice(c * NCHUNK, (c + 1) * NCHUNK)
            gu = _gelu_exact(proj(u0 + c * NCHUNK, NCHUNK))
            acc = proj(gg0 + c * NCHUNK, NCHUNK)
            m_ref[:, ccols] = (gu * mix_ref[:, ccols] * (acc * _sigmoid(acc))).astype(_BF16)

    def attn_q():
        for c in range(nsplit):
            acc = proj(q0 + c * NCHUNK, NCHUNK)
            for hd in range(heads_per_chunk):
                col = c * NCHUNK + hd * HEAD_DIM
                q_ref[:, col:col + HEAD_DIM] = (rope(
                    acc[:, hd * HEAD_DIM:(hd + 1) * HEAD_DIM]) * Q_SCALE).astype(_BF16)

    def attn_gate():
        for c in range(nsplit):
            acc = proj(ga0 + c * NCHUNK, NCHUNK)
            ga_ref[:, c * NCHUNK:(c + 1) * NCHUNK] = (acc * _sigmoid(acc)).astype(_BF16)

    def attn_kv():
        acc = proj(k0, KV_WIDTH)
        for hd in range(N_KV):
            k_ref[:, hd * HEAD_DIM:(hd + 1) * HEAD_DIM] = rope(
                acc[:, hd * HEAD_DIM:(hd + 1) * HEAD_DIM]).astype(_BF16)
        v_ref[...] = proj(k0 + KV_WIDTH, KV_WIDTH).astype(_BF16)

    gmlp_v()
    attn_q()
    layer_norm()
    spatial_mix()
    gmlp_gate()
    attn_gate()
    attn_kv()


def _in_proj(x2d, seq, pre_g, w_in_bf, rope_row, rope_base, ln_g, ln_b, ws_bf, bs_b):
    m = x2d.shape[0]
    tm = TM_IN
    assert m % tm == 0 and seq % tm == 0 and tm % CHUNK == 0
    blocks_per_seq = seq // tm
    const = lambda i: (0, 0)
    const3 = lambda i: (0, 0, 0)
    row = lambda i: (i, 0)
    resident = dict(pipeline_mode=pl.Buffered(1))
    out_widths = (ATTN_WIDTH, KV_WIDTH, KV_WIDTH, ATTN_WIDTH, GMLP_WIDTH)
    return pl.pallas_call(
        _in_proj_kernel,
        grid=(m // tm,),
        in_specs=[
            pl.BlockSpec((tm, D_MODEL), row),
            pl.BlockSpec((1, D_MODEL), const),
            pl.BlockSpec((D_MODEL, IN_WIDTH), const, **resident),
            pl.BlockSpec((2, tm, HEAD_DIM), const3),
            pl.BlockSpec((blocks_per_seq, 8, HEAD_DIM), const3),
            pl.BlockSpec((1, GMLP_WIDTH), const),
            pl.BlockSpec((1, GMLP_WIDTH), const),
            pl.BlockSpec((N_GMLP_HEADS, CHUNK, CHUNK), const3, **resident),
            pl.BlockSpec((N_GMLP_HEADS, CHUNK, HEAD_DIM), const3, **resident),
        ],
        out_specs=[pl.BlockSpec((tm, w), row) for w in out_widths],
        out_shape=[jax.ShapeDtypeStruct((m, w), _BF16) for w in out_widths],
        scratch_shapes=[pltpu.VMEM((tm, D_MODEL), _BF16),
                        pltpu.VMEM((tm, GMLP_WIDTH), _F32),
                        pltpu.VMEM((tm, GMLP_WIDTH), _BF16)],
        compiler_params=pltpu.CompilerParams(
            dimension_semantics=("arbitrary",),
            vmem_limit_bytes=VMEM_LIMIT_BYTES),
        name="in_proj",
    )(x2d, pre_g, w_in_bf, rope_row, rope_base, ln_g, ln_b, ws_bf, bs_b)


def _attn_kernel(seq, ncast, sink_ref, q_ref, kp_ref, kc_ref, kn_ref,
                 vp_ref, vc_ref, vn_ref, ga_ref, *rest):
    o_ref = rest[ncast]
    for src_ref, dst_ref in zip(rest[:ncast], rest[ncast + 1:]):
        dst_ref[...] = src_ref[...].astype(_BF16)

    n = pl.program_id(1)
    tq = q_ref.shape[0]
    nsub = tq // BLOCK

    ik = lax.broadcasted_iota(jnp.int32, (3 * BLOCK, BLOCK), 0)
    iq = lax.broadcasted_iota(jnp.int32, (3 * BLOCK, BLOCK), 1)
    band_t = jnp.abs(iq + BLOCK - ik) <= WINDOW
    eye = (lax.broadcasted_iota(jnp.int32, (BLOCK, BLOCK), 0)
           == lax.broadcasted_iota(jnp.int32, (BLOCK, BLOCK), 1)).astype(_BF16)
    onehot = jnp.concatenate([eye] * GROUP, axis=0)

    for j in range(nsub):
        kbase = n * tq + (j - 1) * BLOCK
        in_seq = (ik + kbase >= 0) & (ik + kbase < seq)
        bias_t = jnp.where(band_t & in_seq, 0.0, NEG_INF).astype(_BF16)
        for g in range(N_KV):
            lanes = slice(g * HEAD_DIM, (g + 1) * HEAD_DIM)

            def kv_rows(prev_ref, cur_ref, next_ref):
                lo = prev_ref[:, lanes] if j == 0 else cur_ref[(j - 1) * BLOCK:j * BLOCK, lanes]
                mid = cur_ref[j * BLOCK:(j + 1) * BLOCK, lanes]
                hi = (next_ref[:, lanes] if j == nsub - 1
                      else cur_ref[(j + 1) * BLOCK:(j + 2) * BLOCK, lanes])
                return jnp.concatenate([lo, mid, hi], axis=0)

            kk = jnp.concatenate([kv_rows(kp_ref, kc_ref, kn_ref), bias_t], axis=1)
            vv = jnp.concatenate(
                [kv_rows(vp_ref, vc_ref, vn_ref),
                 jnp.ones((3 * BLOCK, HEAD_DIM), _BF16)], axis=1)
            qs = jnp.concatenate(
                [q_ref[j * BLOCK:(j + 1) * BLOCK,
                       (g * GROUP + r) * HEAD_DIM:(g * GROUP + r + 1) * HEAD_DIM]
                 for r in range(GROUP)], axis=0)
            s = lax.dot_general(jnp.concatenate([qs, onehot], axis=1), kk,
                                (((1,), (1,)), ((), ())),
                                preferred_element_type=_F32)
            sink = jnp.concatenate(
                [jnp.full((BLOCK, HEAD_DIM), sink_ref[g * GROUP + r] * LOG2E, _F32)
                 for r in range(GROUP)], axis=0)
            sb = [s[:, i * BLOCK:(i + 1) * BLOCK] for i in range(3)]
            rowmax = jnp.max(jnp.maximum(jnp.maximum(sb[0], sb[1]), sb[2]),
                             axis=-1, keepdims=True)
            mx = jnp.maximum(jnp.broadcast_to(rowmax, (GROUP * BLOCK, HEAD_DIM)), sink)
            p = jnp.concatenate([jnp.exp2(t - mx) for t in sb], axis=1).astype(_BF16)
            o = jnp.dot(p, vv, preferred_element_type=_F32)
            denom = o[:, HEAD_DIM:] + jnp.exp2(sink - mx)
            o = o[:, :HEAD_DIM] * (1.0 / denom)
            for r in range(GROUP):
                cols = slice((g * GROUP + r) * HEAD_DIM, (g * GROUP + r + 1) * HEAD_DIM)
                gate = ga_ref[j * BLOCK:(j + 1) * BLOCK, cols].astype(_F32)
                o_ref[j * BLOCK:(j + 1) * BLOCK, cols] = (
                    o[r * BLOCK:(r + 1) * BLOCK, :] * gate).astype(_BF16)


def _attn(q, k, v, ga, sink, batch, seq, cast_weights=()):
    tq = TQ
    assert seq % tq == 0
    sub = tq // BLOCK
    nblk = seq // BLOCK
    nq = seq // tq
    nsteps = batch * nq
    slab = lambda b, n: (b * nq + n, 0)
    cast_specs = []
    for w in cast_weights:
        rows = w.shape[0] // nsteps
        assert w.shape[0] % nsteps == 0 and rows % 16 == 0
        cast_specs.append(pl.BlockSpec((rows, w.shape[1]), slab))
    q3 = q.reshape(batch, seq, ATTN_WIDTH)
    k3 = k.reshape(batch, seq, KV_WIDTH)
    v3 = v.reshape(batch, seq, KV_WIDTH)
    ga3 = ga.reshape(batch, seq, ATTN_WIDTH)
    cur = lambda b, n: (b, n, 0)
    prev = lambda b, n: (b, jnp.maximum(n * sub - 1, 0), 0)
    nxt = lambda b, n: (b, jnp.minimum(n * sub + sub, nblk - 1), 0)
    kv_specs = [
        pl.BlockSpec((None, BLOCK, KV_WIDTH), prev),
        pl.BlockSpec((None, tq, KV_WIDTH), cur),
        pl.BlockSpec((None, BLOCK, KV_WIDTH), nxt),
    ]
    outs = pl.pallas_call(
        functools.partial(_attn_kernel, seq, len(cast_weights)),
        grid=(batch, nq),
        in_specs=[pl.BlockSpec(memory_space=pltpu.SMEM),
                  pl.BlockSpec((None, tq, ATTN_WIDTH), cur)]
                 + kv_specs + kv_specs
                 + [pl.BlockSpec((None, tq, ATTN_WIDTH), cur)]
                 + cast_specs,
        out_specs=[pl.BlockSpec((None, tq, ATTN_WIDTH), cur)] + cast_specs,
        out_shape=[jax.ShapeDtypeStruct((batch, seq, ATTN_WIDTH), _BF16)]
                  + [jax.ShapeDtypeStruct(w.shape, _BF16) for w in cast_weights],
        compiler_params=pltpu.CompilerParams(
            dimension_semantics=("arbitrary", "arbitrary"),
            vmem_limit_bytes=VMEM_LIMIT_BYTES),
        name="window_attn",
    )(sink, q3, k3, k3, k3, v3, v3, v3, ga3, *cast_weights)
    return (outs[0].reshape(batch * seq, ATTN_WIDTH), *outs[1:])


def _out_proj_kernel(a_ref, m_ref, x_ref, p_ref, wout_ref, postg_ref, wpe_ref, wpg_ref, o_ref):
    tm = x_ref.shape[0]
    subs = [slice(t * SUB_OUT, (t + 1) * SUB_OUT) for t in range(tm // SUB_OUT)]

    def mix_proj(sub):
        am = jnp.concatenate([a_ref[sub, :], m_ref[sub, :]], axis=1)
        return jnp.dot(am, wout_ref[...], preferred_element_type=_F32)

    def post_norm(sub, y):
        ms = jnp.mean(y * y, axis=-1, keepdims=True)
        return x_ref[sub, :] + y * lax.rsqrt(ms + EPS) * postg_ref[...]

    def ple(sub, x1):
        x1b = x1.astype(_BF16)
        pb = p_ref[sub, :].astype(_BF16)
        for c0, c1 in PLE_COLUMN_PIECES:
            gate = _sigmoid(jnp.dot(x1b, wpg_ref[:, c0:c1], preferred_element_type=_F32))
            pe = jnp.dot(pb, wpe_ref[:, c0:c1], preferred_element_type=_F32)
            o_ref[sub, c0:c1] = x1[:, c0:c1] + gate * pe

    ys = [mix_proj(sub) for sub in subs]
    for sub, y in zip(subs, ys):
        ple(sub, post_norm(sub, y))


def _out_proj(a, m_gated, x2d, p2d, w_out_bf, post_g, w_pe_bf, w_pg_bf):
    m = x2d.shape[0]
    tm = TM_OUT
    assert m % tm == 0 and tm % SUB_OUT == 0
    row = lambda i: (i, 0)
    const2 = lambda i: (0, 0)
    resident = dict(pipeline_mode=pl.Buffered(1))
    return pl.pallas_call(
        _out_proj_kernel,
        grid=(m // tm,),
        in_specs=[
            pl.BlockSpec((tm, ATTN_WIDTH), row),
            pl.BlockSpec((tm, GMLP_WIDTH), row),
            pl.BlockSpec((tm, D_MODEL), row),
            pl.BlockSpec((tm, PLE_DIM), row),
            pl.BlockSpec((D_MODEL, D_MODEL), const2, **resident),
            pl.BlockSpec((1, D_MODEL), const2),
            pl.BlockSpec((PLE_DIM, D_MODEL), const2, **resident),
            pl.BlockSpec((D_MODEL, D_MODEL), const2, **resident),
        ],
        out_specs=pl.BlockSpec((tm, D_MODEL), row),
        out_shape=jax.ShapeDtypeStruct((m, D_MODEL), _F32),
        compiler_params=pltpu.CompilerParams(
            dimension_semantics=("arbitrary",),
            vmem_limit_bytes=VMEM_LIMIT_BYTES),
        name="out_proj",
    )(a, m_gated, x2d, p2d, w_out_bf, post_g, w_pe_bf, w_pg_bf)


def _rope_tables(max_seq, tm):
    inv = 1.0 / (ROPE_THETA ** (jnp.arange(0, HEAD_DIM, 2, dtype=_F32) / HEAD_DIM))
    inv2 = jnp.concatenate([inv, inv])[None, :]
    ang_row = jnp.arange(tm, dtype=_F32)[:, None] * inv2
    ang_base = jnp.arange(0, max_seq, tm, dtype=_F32)[:, None] * inv2
    nb = max_seq // tm
    sign = jnp.broadcast_to(
        jnp.where(jnp.arange(HEAD_DIM) < HEAD_DIM // 2, -1.0, 1.0).astype(_F32)[None, :],
        (nb, HEAD_DIM))
    rope_row = jnp.stack([jnp.cos(ang_row), jnp.sin(ang_row)])
    rope_base = jnp.stack([jnp.cos(ang_base), jnp.sin(ang_base), sign]
                          + [jnp.zeros((nb, HEAD_DIM), _F32)] * 5, axis=1)
    return rope_row, rope_base


def _layer(x, p, params, tables, out_weights_f32=None, out_weights_bf=None):
    (pre_g, w_in_bf, sink, ln_g, ln_b, ws_bf, bs_b, post_g) = params
    batch, seq, _ = x.shape
    rope_row, rope_base = tables
    x2d = x.reshape(batch * seq, D_MODEL)
    p2d = p.reshape(batch * seq, PLE_DIM)
    q, k, v, ga, m_gated = _in_proj(x2d, seq, pre_g, w_in_bf, rope_row, rope_base,
                                    ln_g, ln_b, ws_bf, bs_b)
    if out_weights_bf is None:
        a, *out_weights_bf = _attn(q, k, v, ga, sink, batch, seq, cast_weights=out_weights_f32)
    else:
        a, = _attn(q, k, v, ga, sink, batch, seq)
    w_out_bf, w_pg_bf, w_pe_bf = out_weights_bf
    out = _out_proj(a, m_gated, x2d, p2d, w_out_bf, post_g, w_pe_bf, w_pg_bf)
    return out.reshape(batch, seq, D_MODEL), out_weights_bf


def kernel(x_prompt, x_sample, p_prompt, p_sample, pre_norm_g, w_in, attn_sink,
           gmlp_ln_g, gmlp_ln_b, gmlp_ws, gmlp_bs, w_out, post_norm_g, w_pe, w_pg):
    depth = w_in.shape[0]
    tables = _rope_tables(max(x_prompt.shape[1], x_sample.shape[1]), TM_IN)
    y_prompt, y_sample = x_prompt, x_sample
    for i in range(depth):
        params = (
            pre_norm_g[i].reshape(1, D_MODEL),
            w_in[i].astype(_BF16),
            attn_sink[i],
            gmlp_ln_g[i].reshape(1, GMLP_WIDTH),
            gmlp_ln_b[i].reshape(1, GMLP_WIDTH),
            gmlp_ws[i].astype(_BF16),
            jnp.broadcast_to(gmlp_bs[i][:, :, None], (N_GMLP_HEADS, CHUNK, HEAD_DIM)),
            post_norm_g[i].reshape(1, D_MODEL),
        )
        y_prompt, out_weights_bf = _layer(y_prompt, p_prompt[i], params, tables,
                                          out_weights_f32=(w_out[i], w_pg[i], w_pe[i]))
        y_sample, _ = _layer(y_sample, p_sample[i], params, tables,
                             out_weights_bf=out_weights_bf)
    return (y_prompt, y_sample)
```

```python
import functools
import math

import jax
import jax.numpy as jnp
from jax import lax
from jax.experimental import pallas as pl
from jax.experimental.pallas import tpu as pltpu

D_MODEL = 2048
HEAD_DIM = 128
ATTN_WIDTH = 1024
N_HEADS = 8
N_KV = 2
GROUP = 4
KV_WIDTH = 256
WINDOW = 128
BLOCK = 128
ROPE_THETA = 10000.0
GMLP_WIDTH = 1024
N_GMLP_HEADS = 8
CHUNK = 128
PLE_DIM = 256
EPS = 1e-6
IN_WIDTH = 5632
NEG_INF = -1e30
LOG2E = math.log2(math.e)
Q_SCALE = HEAD_DIM ** -0.5 * LOG2E

VMEM_LIMIT_BYTES = 56 * 1024 * 1024

TM_IN = 512
SUB_IN = 256
NCHUNK = 512
TQ = 2048
TM_OUT = 512
SUB_OUT = 256
PLE_COLUMN_PIECES = ((0, 1024), (1024, 1792), (1792, 2048))

_BF16 = jnp.bfloat16
_F32 = jnp.float32


def _sigmoid(x):
    return 1.0 / (1.0 + jnp.exp(-x))


def _gelu_exact(x):
    return 0.5 * x * (1.0 + lax.erf(x * (2.0 ** -0.5)))


def _in_proj_kernel(x_ref, g_ref, w_ref, rope_cos_ref, rope_sin_ref, lng_ref, lnb_ref, ws_ref,
                    bsb_ref,
                    q_ref, k_ref, v_ref, ga_ref, m_ref, h_ref, vgf_ref, vn_ref):
    tm = x_ref.shape[0]
    rows = 64
    for r in range(tm // rows):
        x = x_ref[r * rows:(r + 1) * rows, :]
        ms = jnp.mean(x * x, axis=-1, keepdims=True)
        h = x * lax.rsqrt(ms + EPS) * g_ref[...]
        h_ref[r * rows:(r + 1) * rows, :] = h.astype(_BF16)

    cos = rope_cos_ref[...]
    sin = rope_sin_ref[...]

    def rope(t):
        return t * cos + pltpu.roll(t, HEAD_DIM // 2, 1) * sin

    def proj(start, width):
        return jnp.concatenate(
            [jnp.dot(h_ref[r * SUB_IN:(r + 1) * SUB_IN, :], w_ref[:, start:start + width],
                     preferred_element_type=_F32) for r in range(tm // SUB_IN)], axis=0)

    q0, k0, ga0 = 0, ATTN_WIDTH, ATTN_WIDTH + 2 * KV_WIDTH
    u0 = ga0 + ATTN_WIDTH
    vg0 = u0 + GMLP_WIDTH
    gg0 = vg0 + GMLP_WIDTH
    nsplit = ATTN_WIDTH // NCHUNK
    heads_per_chunk = NCHUNK // HEAD_DIM
    nchunks = tm // CHUNK
    mix_ref = vgf_ref

    def gmlp_v():
        for c in range(nsplit):
            ccols = slice(c * NCHUNK, (c + 1) * NCHUNK)
            vgf_ref[:, ccols] = _gelu_exact(proj(vg0 + c * NCHUNK, NCHUNK))

    def layer_norm():
        for r in range(tm // rows):
            rs = slice(r * rows, (r + 1) * rows)
            vf = vgf_ref[rs, :]
            mu = jnp.mean(vf, axis=-1, keepdims=True)
            vc = vf - mu
            var = jnp.mean(vc * vc, axis=-1, keepdims=True)
            vn = vc * lax.rsqrt(var + EPS) * lng_ref[...] + lnb_ref[...]
            vn_ref[rs, :] = vn.astype(_BF16)

    def spatial_mix():
        for h in range(N_GMLP_HEADS):
            cols = slice(h * HEAD_DIM, (h + 1) * HEAD_DIM)
            rhs = jnp.concatenate(
                [vn_ref[rc * CHUNK:(rc + 1) * CHUNK, cols] for rc in range(nchunks)], axis=1)
            mixed = jnp.dot(ws_ref[h], rhs, preferred_element_type=_F32)
            for rc in range(nchunks):
                mix_ref[rc * CHUNK:(rc + 1) * CHUNK, cols] = (
                    mixed[:, rc * HEAD_DIM:(rc + 1) * HEAD_DIM] + bsb_ref[h])

    def gmlp_gate():
        for c in range(nsplit):
            ccols = slice(c * NCHUNK, (c + 1) * NCHUNK)
            gu = _gelu_exact(proj(u0 + c * NCHUNK, NCHUNK))
            acc = proj(gg0 + c * NCHUNK, NCHUNK)
            m_ref[:, ccols] = (gu * mix_ref[:, ccols] * (acc * _sigmoid(acc))).astype(_BF16)

    def attn_q():
        for c in range(nsplit):
            acc = proj(q0 + c * NCHUNK, NCHUNK)
            for hd in range(heads_per_chunk):
                col = c * NCHUNK + hd * HEAD_DIM
                q_ref[:, col:col + HEAD_DIM] = (rope(
                    acc[:, hd * HEAD_DIM:(hd + 1) * HEAD_DIM]) * Q_SCALE).astype(_BF16)

    def attn_gate():
        for c in range(nsplit):
            acc = proj(ga0 + c * NCHUNK, NCHUNK)
            ga_ref[:, c * NCHUNK:(c + 1) * NCHUNK] = (acc * _sigmoid(acc)).astype(_BF16)

    def attn_kv():
        acc = proj(k0, KV_WIDTH)
        for hd in range(N_KV):
            k_ref[:, hd * HEAD_DIM:(hd + 1) * HEAD_DIM] = rope(
                acc[:, hd * HEAD_DIM:(hd + 1) * HEAD_DIM]).astype(_BF16)
        v_ref[...] = proj(k0 + KV_WIDTH, KV_WIDTH).astype(_BF16)

    gmlp_v()
    attn_q()
    layer_norm()
    spatial_mix()
    gmlp_gate()
    attn_gate()
    attn_kv()


def _in_proj(x2d, seq, pre_g, w_in_bf, rope_cos, rope_sin, ln_g, ln_b, ws_bf, bs_b):
    m = x2d.shape[0]
    tm = TM_IN
    assert m % tm == 0 and seq % tm == 0 and tm % CHUNK == 0
    blocks_per_seq = seq // tm
    const = lambda i: (0, 0)
    const3 = lambda i: (0, 0, 0)
    row = lambda i: (i, 0)
    resident = dict(pipeline_mode=pl.Buffered(1))
    out_widths = (ATTN_WIDTH, KV_WIDTH, KV_WIDTH, ATTN_WIDTH, GMLP_WIDTH)
    return pl.pallas_call(
        _in_proj_kernel,
        grid=(m // tm,),
        in_specs=[
            pl.BlockSpec((tm, D_MODEL), row),
            pl.BlockSpec((1, D_MODEL), const),
            pl.BlockSpec((D_MODEL, IN_WIDTH), const, **resident),
            pl.BlockSpec((tm, HEAD_DIM), lambda i: (i % blocks_per_seq, 0)),
            pl.BlockSpec((tm, HEAD_DIM), lambda i: (i % blocks_per_seq, 0)),
            pl.BlockSpec((1, GMLP_WIDTH), const),
            pl.BlockSpec((1, GMLP_WIDTH), const),
            pl.BlockSpec((N_GMLP_HEADS, CHUNK, CHUNK), const3, **resident),
            pl.BlockSpec((N_GMLP_HEADS, CHUNK, HEAD_DIM), const3, **resident),
        ],
        out_specs=[pl.BlockSpec((tm, w), row) for w in out_widths],
        out_shape=[jax.ShapeDtypeStruct((m, w), _BF16) for w in out_widths],
        scratch_shapes=[pltpu.VMEM((tm, D_MODEL), _BF16),
                        pltpu.VMEM((tm, GMLP_WIDTH), _F32),
                        pltpu.VMEM((tm, GMLP_WIDTH), _BF16)],
        compiler_params=pltpu.CompilerParams(
            dimension_semantics=("arbitrary",),
            vmem_limit_bytes=VMEM_LIMIT_BYTES),
        name="in_proj",
    )(x2d, pre_g, w_in_bf, rope_cos, rope_sin, ln_g, ln_b, ws_bf, bs_b)


def _attn_kernel(seq, ncast, sink_ref, q_ref, kp_ref, kc_ref, kn_ref,
                 vp_ref, vc_ref, vn_ref, ga_ref, *rest):
    o_ref = rest[ncast]
    for src_ref, dst_ref in zip(rest[:ncast], rest[ncast + 1:]):
        dst_ref[...] = src_ref[...].astype(_BF16)

    n = pl.program_id(1)
    tq = q_ref.shape[0]
    nsub = tq // BLOCK

    ik = lax.broadcasted_iota(jnp.int32, (3 * BLOCK, BLOCK), 0)
    iq = lax.broadcasted_iota(jnp.int32, (3 * BLOCK, BLOCK), 1)
    band_t = jnp.abs(iq + BLOCK - ik) <= WINDOW
    eye = (lax.broadcasted_iota(jnp.int32, (BLOCK, BLOCK), 0)
           == lax.broadcasted_iota(jnp.int32, (BLOCK, BLOCK), 1)).astype(_BF16)
    onehot = jnp.concatenate([eye] * GROUP, axis=0)

    for j in range(nsub):
        kbase = n * tq + (j - 1) * BLOCK
        in_seq = (ik + kbase >= 0) & (ik + kbase < seq)
        bias_t = jnp.where(band_t & in_seq, 0.0, NEG_INF).astype(_BF16)
        for g in range(N_KV):
            lanes = slice(g * HEAD_DIM, (g + 1) * HEAD_DIM)

            def kv_rows(prev_ref, cur_ref, next_ref):
                lo = prev_ref[:, lanes] if j == 0 else cur_ref[(j - 1) * BLOCK:j * BLOCK, lanes]
                mid = cur_ref[j * BLOCK:(j + 1) * BLOCK, lanes]
                hi = (next_ref[:, lanes] if j == nsub - 1
                      else cur_ref[(j + 1) * BLOCK:(j + 2) * BLOCK, lanes])
                return jnp.concatenate([lo, mid, hi], axis=0)

            kk = jnp.concatenate([kv_rows(kp_ref, kc_ref, kn_ref), bias_t], axis=1)
            vv = jnp.concatenate(
                [kv_rows(vp_ref, vc_ref, vn_ref),
                 jnp.ones((3 * BLOCK, HEAD_DIM), _BF16)], axis=1)
            qs = jnp.concatenate(
                [q_ref[j * BLOCK:(j + 1) * BLOCK,
                       (g * GROUP + r) * HEAD_DIM:(g * GROUP + r + 1) * HEAD_DIM]
                 for r in range(GROUP)], axis=0)
            s = lax.dot_general(jnp.concatenate([qs, onehot], axis=1), kk,
                                (((1,), (1,)), ((), ())),
                                preferred_element_type=_F32)
            sink = jnp.concatenate(
                [jnp.full((BLOCK, HEAD_DIM), sink_ref[g * GROUP + r] * LOG2E, _F32)
                 for r in range(GROUP)], axis=0)
            sb = [s[:, i * BLOCK:(i + 1) * BLOCK] for i in range(3)]
            rowmax = jnp.max(jnp.maximum(jnp.maximum(sb[0], sb[1]), sb[2]),
                             axis=-1, keepdims=True)
            mx = jnp.maximum(jnp.broadcast_to(rowmax, (GROUP * BLOCK, HEAD_DIM)), sink)
            p = jnp.concatenate([jnp.exp2(t - mx) for t in sb], axis=1).astype(_BF16)
            o = jnp.dot(p, vv, preferred_element_type=_F32)
            denom = o[:, HEAD_DIM:] + jnp.exp2(sink - mx)
            o = o[:, :HEAD_DIM] * (1.0 / denom)
            for r in range(GROUP):
                cols = slice((g * GROUP + r) * HEAD_DIM, (g * GROUP + r + 1) * HEAD_DIM)
                gate = ga_ref[j * BLOCK:(j + 1) * BLOCK, cols].astype(_F32)
                o_ref[j * BLOCK:(j + 1) * BLOCK, cols] = (
                    o[r * BLOCK:(r + 1) * BLOCK, :] * gate).astype(_BF16)


def _attn(q, k, v, ga, sink, batch, seq, cast_weights=()):
    tq = TQ
    assert seq % tq == 0
    sub = tq // BLOCK
    nblk = seq // BLOCK
    nq = seq // tq
    nsteps = batch * nq
    slab = lambda b, n: (b * nq + n, 0)
    cast_specs = []
    for w in cast_weights:
        rows = w.shape[0] // nsteps
        assert w.shape[0] % nsteps == 0 and rows % 16 == 0
        cast_specs.append(pl.BlockSpec((rows, w.shape[1]), slab))
    q3 = q.reshape(batch, seq, ATTN_WIDTH)
    k3 = k.reshape(batch, seq, KV_WIDTH)
    v3 = v.reshape(batch, seq, KV_WIDTH)
    ga3 = ga.reshape(batch, seq, ATTN_WIDTH)
    cur = lambda b, n: (b, n, 0)
    prev = lambda b, n: (b, jnp.maximum(n * sub - 1, 0), 0)
    nxt = lambda b, n: (b, jnp.minimum(n * sub + sub, nblk - 1), 0)
    kv_specs = [
        pl.BlockSpec((None, BLOCK, KV_WIDTH), prev),
        pl.BlockSpec((None, tq, KV_WIDTH), cur),
        pl.BlockSpec((None, BLOCK, KV_WIDTH), nxt),
    ]
    outs = pl.pallas_call(
        functools.partial(_attn_kernel, seq, len(cast_weights)),
        grid=(batch, nq),
        in_specs=[pl.BlockSpec(memory_space=pltpu.SMEM),
                  pl.BlockSpec((None, tq, ATTN_WIDTH), cur)]
                 + kv_specs + kv_specs
                 + [pl.BlockSpec((None, tq, ATTN_WIDTH), cur)]
                 + cast_specs,
        out_specs=[pl.BlockSpec((None, tq, ATTN_WIDTH), cur)] + cast_specs,
        out_shape=[jax.ShapeDtypeStruct((batch, seq, ATTN_WIDTH), _BF16)]
                  + [jax.ShapeDtypeStruct(w.shape, _BF16) for w in cast_weights],
        compiler_params=pltpu.CompilerParams(
            dimension_semantics=("arbitrary", "arbitrary"),
            vmem_limit_bytes=VMEM_LIMIT_BYTES),
        name="window_attn",
    )(sink, q3, k3, k3, k3, v3, v3, v3, ga3, *cast_weights)
    return (outs[0].reshape(batch * seq, ATTN_WIDTH), *outs[1:])


def _out_proj_kernel(a_ref, m_ref, x_ref, p_ref, wout_ref, postg_ref, wpe_ref, wpg_ref, o_ref):
    tm = x_ref.shape[0]
    subs = [slice(t * SUB_OUT, (t + 1) * SUB_OUT) for t in range(tm // SUB_OUT)]

    def mix_proj(sub):
        am = jnp.concatenate([a_ref[sub, :], m_ref[sub, :]], axis=1)
        return jnp.dot(am, wout_ref[...], preferred_element_type=_F32)

    def post_norm(sub, y):
        ms = jnp.mean(y * y, axis=-1, keepdims=True)
        return x_ref[sub, :] + y * lax.rsqrt(ms + EPS) * postg_ref[...]

    def ple(sub, x1):
        x1b = x1.astype(_BF16)
        pb = p_ref[sub, :].astype(_BF16)
        for c0, c1 in PLE_COLUMN_PIECES:
            gate = _sigmoid(jnp.dot(x1b, wpg_ref[:, c0:c1], preferred_element_type=_F32))
            pe = jnp.dot(pb, wpe_ref[:, c0:c1], preferred_element_type=_F32)
            o_ref[sub, c0:c1] = x1[:, c0:c1] + gate * pe

    ys = [mix_proj(sub) for sub in subs]
    for sub, y in zip(subs, ys):
        ple(sub, post_norm(sub, y))


def _out_proj(a, m_gated, x2d, p2d, w_out_bf, post_g, w_pe_bf, w_pg_bf):
    m = x2d.shape[0]
    tm = TM_OUT
    assert m % tm == 0 and tm % SUB_OUT == 0
    row = lambda i: (i, 0)
    const2 = lambda i: (0, 0)
    resident = dict(pipeline_mode=pl.Buffered(1))
    return pl.pallas_call(
        _out_proj_kernel,
        grid=(m // tm,),
        in_specs=[
            pl.BlockSpec((tm, ATTN_WIDTH), row),
            pl.BlockSpec((tm, GMLP_WIDTH), row),
            pl.BlockSpec((tm, D_MODEL), row),
            pl.BlockSpec((tm, PLE_DIM), row),
            pl.BlockSpec((D_MODEL, D_MODEL), const2, **resident),
            pl.BlockSpec((1, D_MODEL), const2),
            pl.BlockSpec((PLE_DIM, D_MODEL), const2, **resident),
            pl.BlockSpec((D_MODEL, D_MODEL), const2, **resident),
        ],
        out_specs=pl.BlockSpec((tm, D_MODEL), row),
        out_shape=jax.ShapeDtypeStruct((m, D_MODEL), _F32),
        compiler_params=pltpu.CompilerParams(
            dimension_semantics=("arbitrary",),
            vmem_limit_bytes=VMEM_LIMIT_BYTES),
        name="out_proj",
    )(a, m_gated, x2d, p2d, w_out_bf, post_g, w_pe_bf, w_pg_bf)


def _rope_tables(max_seq, tm):
    inv = 1.0 / (ROPE_THETA ** (jnp.arange(0, HEAD_DIM, 2, dtype=_F32) / HEAD_DIM))
    inv2 = jnp.concatenate([inv, inv])[None, :]
    ang_row = jnp.arange(tm, dtype=_F32)[:, None] * inv2
    ang_base = jnp.arange(0, max_seq, tm, dtype=_F32)[:, None] * inv2
    cos_r, sin_r = jnp.cos(ang_row)[None], jnp.sin(ang_row)[None]
    cos_b, sin_b = jnp.cos(ang_base)[:, None], jnp.sin(ang_base)[:, None]
    sign = jnp.where(jnp.arange(HEAD_DIM) < HEAD_DIM // 2, -1.0, 1.0).astype(_F32)
    cos = (cos_r * cos_b - sin_r * sin_b).reshape(max_seq, HEAD_DIM)
    sin = ((sin_r * cos_b + cos_r * sin_b) * sign).reshape(max_seq, HEAD_DIM)
    return cos, sin


def _layer(x, p, params, tables, out_weights_f32=None, out_weights_bf=None):
    (pre_g, w_in_bf, sink, ln_g, ln_b, ws_bf, bs_b, post_g) = params
    batch, seq, _ = x.shape
    rope_cos, rope_sin = tables
    x2d = x.reshape(batch * seq, D_MODEL)
    p2d = p.reshape(batch * seq, PLE_DIM)
    q, k, v, ga, m_gated = _in_proj(x2d, seq, pre_g, w_in_bf, rope_cos, rope_sin,
                                    ln_g, ln_b, ws_bf, bs_b)
    if out_weights_bf is None:
        a, *out_weights_bf = _attn(q, k, v, ga, sink, batch, seq, cast_weights=out_weights_f32)
    else:
        a, = _attn(q, k, v, ga, sink, batch, seq)
    w_out_bf, w_pg_bf, w_pe_bf = out_weights_bf
    out = _out_proj(a, m_gated, x2d, p2d, w_out_bf, post_g, w_pe_bf, w_pg_bf)
    return out.reshape(batch, seq, D_MODEL), out_weights_bf


def kernel(x_prompt, x_sample, p_prompt, p_sample, pre_norm_g, w_in, attn_sink,
           gmlp_ln_g, gmlp_ln_b, gmlp_ws, gmlp_bs, w_out, post_norm_g, w_pe, w_pg):
    depth = w_in.shape[0]
    tables = _rope_tables(max(x_prompt.shape[1], x_sample.shape[1]), TM_IN)
    y_prompt, y_sample = x_prompt, x_sample
    for i in range(depth):
        params = (
            pre_norm_g[i].reshape(1, D_MODEL),
            w_in[i].astype(_BF16),
            attn_sink[i],
            gmlp_ln_g[i].reshape(1, GMLP_WIDTH),
            gmlp_ln_b[i].reshape(1, GMLP_WIDTH),
            gmlp_ws[i].astype(_BF16),
            jnp.broadcast_to(gmlp_bs[i][:, :, None], (N_GMLP_HEADS, CHUNK, HEAD_DIM)),
            post_norm_g[i].reshape(1, D_MODEL),
        )
        y_prompt, out_weights_bf = _layer(y_prompt, p_prompt[i], params, tables,
                                          out_weights_f32=(w_out[i], w_pg[i], w_pe[i]))
        y_sample, _ = _layer(y_sample, p_sample[i], params, tables,
                             out_weights_bf=out_weights_bf)
    return (y_prompt, y_sample)
```

```python
import functools
import math

import jax
import jax.numpy as jnp
from jax import lax
from jax.experimental import pallas as pl
from jax.experimental.pallas import tpu as pltpu

D_MODEL = 2048
HEAD_DIM = 128
ATTN_WIDTH = 1024
N_HEADS = 8
N_KV = 2
GROUP = 4
KV_WIDTH = 256
WINDOW = 128
BLOCK = 128
ROPE_THETA = 10000.0
GMLP_WIDTH = 1024
N_GMLP_HEADS = 8
CHUNK = 128
PLE_DIM = 256
EPS = 1e-6
IN_WIDTH = 5632
NEG_INF = -1e30
LOG2E = math.log2(math.e)
Q_SCALE = HEAD_DIM ** -0.5 * LOG2E

VMEM_LIMIT_BYTES = 56 * 1024 * 1024

TM_IN = 512
SUB_IN = 256
NCHUNK = 512
TQ = 2048
TM_OUT = 512
SUB_OUT = 256
PLE_COLUMN_PIECES = ((0, 1024), (1024, 1792), (1792, 2048))

_BF16 = jnp.bfloat16
_F32 = jnp.float32


def _sigmoid(x):
    return 1.0 / (1.0 + jnp.exp(-x))


def _gelu_exact(x):
    return 0.5 * x * (1.0 + lax.erf(x * (2.0 ** -0.5)))


def _in_proj_kernel(x_ref, g_ref, w_ref, rope_row_ref, rope_base_ref, lng_ref, lnb_ref, ws_ref,
                    bsb_ref,
                    q_ref, k_ref, v_ref, ga_ref, m_ref, h_ref, vgf_ref, vn_ref):
    tm = x_ref.shape[0]
    rows = 32
    for r in range(tm // rows):
        x = x_ref[r * rows:(r + 1) * rows, :]
        ms = jnp.mean(x * x, axis=-1, keepdims=True)
        h = x * lax.rsqrt(ms + EPS) * g_ref[...]
        h_ref[r * rows:(r + 1) * rows, :] = h.astype(_BF16)

    cos_r, sin_r = rope_row_ref[0], rope_row_ref[1]
    base = rope_base_ref[pl.program_id(0) % rope_base_ref.shape[0]]
    cos_b, sin_b, sign = base[0:1, :], base[1:2, :], base[2:3, :]
    cos = cos_r * cos_b - sin_r * sin_b
    sin = (sin_r * cos_b + cos_r * sin_b) * sign

    def rope(t):
        return t * cos + pltpu.roll(t, HEAD_DIM // 2, 1) * sin

    def proj(start, width):
        return jnp.concatenate(
            [jnp.dot(h_ref[r * SUB_IN:(r + 1) * SUB_IN, :], w_ref[:, start:start + width],
                     preferred_element_type=_F32) for r in range(tm // SUB_IN)], axis=0)

    q0, k0, ga0 = 0, ATTN_WIDTH, ATTN_WIDTH + 2 * KV_WIDTH
    u0 = ga0 + ATTN_WIDTH
    vg0 = u0 + GMLP_WIDTH
    gg0 = vg0 + GMLP_WIDTH
    nsplit = ATTN_WIDTH // NCHUNK
    heads_per_chunk = NCHUNK // HEAD_DIM
    nchunks = tm // CHUNK
    mix_ref = vgf_ref

    def gmlp_v():
        for c in range(nsplit):
            ccols = slice(c * NCHUNK, (c + 1) * NCHUNK)
            vgf_ref[:, ccols] = _gelu_exact(proj(vg0 + c * NCHUNK, NCHUNK))

    def layer_norm():
        for r in range(tm // rows):
            rs = slice(r * rows, (r + 1) * rows)
            vf = vgf_ref[rs, :]
            mu = jnp.mean(vf, axis=-1, keepdims=True)
            vc = vf - mu
            var = jnp.mean(vc * vc, axis=-1, keepdims=True)
            vn = vc * lax.rsqrt(var + EPS) * lng_ref[...] + lnb_ref[...]
            vn_ref[rs, :] = vn.astype(_BF16)

    def spatial_mix():
        for h in range(N_GMLP_HEADS):
            cols = slice(h * HEAD_DIM, (h + 1) * HEAD_DIM)
            rhs = jnp.concatenate(
                [vn_ref[rc * CHUNK:(rc + 1) * CHUNK, cols] for rc in range(nchunks)], axis=1)
            mixed = jnp.dot(ws_ref[h], rhs, preferred_element_type=_F32)
            for rc in range(nchunks):
                mix_ref[rc * CHUNK:(rc + 1) * CHUNK, cols] = (
                    mixed[:, rc * HEAD_DIM:(rc + 1) * HEAD_DIM] + bsb_ref[h])

    def gmlp_gate():
        for c in range(nsplit):
            ccols = slice(c * NCHUNK, (c + 1) * NCHUNK)
            gu = _gelu_exact(proj(u0 + c * NCHUNK, NCHUNK))
            acc = proj(gg0 + c * NCHUNK, NCHUNK)
            m_ref[:, ccols] = (gu * mix_ref[:, ccols] * (acc * _sigmoid(acc))).astype(_BF16)

    def attn_q():
        for c in range(nsplit):
            acc = proj(q0 + c * NCHUNK, NCHUNK)
            for hd in range(heads_per_chunk):
                col = c * NCHUNK + hd * HEAD_DIM
                q_ref[:, col:col + HEAD_DIM] = (rope(
                    acc[:, hd * HEAD_DIM:(hd + 1) * HEAD_DIM]) * Q_SCALE).astype(_BF16)

    def attn_gate():
        for c in range(nsplit):
            acc = proj(ga0 + c * NCHUNK, NCHUNK)
            ga_ref[:, c * NCHUNK:(c + 1) * NCHUNK] = (acc * _sigmoid(acc)).astype(_BF16)

    def attn_kv():
        acc = proj(k0, KV_WIDTH)
        for hd in range(N_KV):
            k_ref[:, hd * HEAD_DIM:(hd + 1) * HEAD_DIM] = rope(
                acc[:, hd * HEAD_DIM:(hd + 1) * HEAD_DIM]).astype(_BF16)
        v_ref[...] = proj(k0 + KV_WIDTH, KV_WIDTH).astype(_BF16)

    gmlp_v()
    attn_q()
    layer_norm()
    spatial_mix()
    gmlp_gate()
    attn_gate()
    attn_kv()


def _in_proj(x2d, seq, pre_g, w_in_bf, rope_row, rope_base, ln_g, ln_b, ws_bf, bs_b):
    m = x2d.shape[0]
    tm = TM_IN
    assert m % tm == 0 and seq % tm == 0 and tm % CHUNK == 0
    blocks_per_seq = seq // tm
    const = lambda i: (0, 0)
    const3 = lambda i: (0, 0, 0)
    row = lambda i: (i, 0)
    resident = dict(pipeline_mode=pl.Buffered(1))
    out_widths = (ATTN_WIDTH, KV_WIDTH, KV_WIDTH, ATTN_WIDTH, GMLP_WIDTH)
    return pl.pallas_call(
        _in_proj_kernel,
        grid=(m // tm,),
        in_specs=[
            pl.BlockSpec((tm, D_MODEL), row),
            pl.BlockSpec((1, D_MODEL), const),
            pl.BlockSpec((D_MODEL, IN_WIDTH), const, **resident),
            pl.BlockSpec((2, tm, HEAD_DIM), const3),
            pl.BlockSpec((blocks_per_seq, 8, HEAD_DIM), const3),
            pl.BlockSpec((1, GMLP_WIDTH), const),
            pl.BlockSpec((1, GMLP_WIDTH), const),
            pl.BlockSpec((N_GMLP_HEADS, CHUNK, CHUNK), const3, **resident),
            pl.BlockSpec((N_GMLP_HEADS, CHUNK, HEAD_DIM), const3, **resident),
        ],
        out_specs=[pl.BlockSpec((tm, w), row) for w in out_widths],
        out_shape=[jax.ShapeDtypeStruct((m, w), _BF16) for w in out_widths],
        scratch_shapes=[pltpu.VMEM((tm, D_MODEL), _BF16),
                        pltpu.VMEM((tm, GMLP_WIDTH), _F32),
                        pltpu.VMEM((tm, GMLP_WIDTH), _BF16)],
        compiler_params=pltpu.CompilerParams(
            dimension_semantics=("arbitrary",),
            vmem_limit_bytes=VMEM_LIMIT_BYTES),
        name="in_proj",
    )(x2d, pre_g, w_in_bf, rope_row, rope_base, ln_g, ln_b, ws_bf, bs_b)


def _attn_kernel(seq, ncast, sink_ref, q_ref, kp_ref, kc_ref, kn_ref,
                 vp_ref, vc_ref, vn_ref, ga_ref, *rest):
    o_ref = rest[ncast]
    for src_ref, dst_ref in zip(rest[:ncast], rest[ncast + 1:]):
        dst_ref[...] = src_ref[...].astype(_BF16)

    n = pl.program_id(1)
    tq = q_ref.shape[0]
    nsub = tq // BLOCK

    ik = lax.broadcasted_iota(jnp.int32, (3 * BLOCK, BLOCK), 0)
    iq = lax.broadcasted_iota(jnp.int32, (3 * BLOCK, BLOCK), 1)
    band_t = jnp.abs(iq + BLOCK - ik) <= WINDOW
    eye = (lax.broadcasted_iota(jnp.int32, (BLOCK, BLOCK), 0)
           == lax.broadcasted_iota(jnp.int32, (BLOCK, BLOCK), 1)).astype(_BF16)
    onehot = jnp.concatenate([eye] * GROUP, axis=0)

    for j in range(nsub):
        kbase = n * tq + (j - 1) * BLOCK
        in_seq = (ik + kbase >= 0) & (ik + kbase < seq)
        bias_t = jnp.where(band_t & in_seq, 0.0, NEG_INF).astype(_BF16)
        for g in range(N_KV):
            lanes = slice(g * HEAD_DIM, (g + 1) * HEAD_DIM)

            def kv_rows(prev_ref, cur_ref, next_ref):
                lo = prev_ref[:, lanes] if j == 0 else cur_ref[(j - 1) * BLOCK:j * BLOCK, lanes]
                mid = cur_ref[j * BLOCK:(j + 1) * BLOCK, lanes]
                hi = (next_ref[:, lanes] if j == nsub - 1
                      else cur_ref[(j + 1) * BLOCK:(j + 2) * BLOCK, lanes])
                return jnp.concatenate([lo, mid, hi], axis=0)

            kk = jnp.concatenate([kv_rows(kp_ref, kc_ref, kn_ref), bias_t], axis=1)
            vv = jnp.concatenate(
                [kv_rows(vp_ref, vc_ref, vn_ref),
                 jnp.ones((3 * BLOCK, HEAD_DIM), _BF16)], axis=1)
            qs = jnp.concatenate(
                [q_ref[j * BLOCK:(j + 1) * BLOCK,
                       (g * GROUP + r) * HEAD_DIM:(g * GROUP + r + 1) * HEAD_DIM]
                 for r in range(GROUP)], axis=0)
            s = lax.dot_general(jnp.concatenate([qs, onehot], axis=1), kk,
                                (((1,), (1,)), ((), ())),
                                preferred_element_type=_F32)
            sink = jnp.concatenate(
                [jnp.full((BLOCK, HEAD_DIM), sink_ref[g * GROUP + r] * LOG2E, _F32)
                 for r in range(GROUP)], axis=0)
            sb = [s[:, i * BLOCK:(i + 1) * BLOCK] for i in range(3)]
            rowmax = jnp.max(jnp.maximum(jnp.maximum(sb[0], sb[1]), sb[2]),
                             axis=-1, keepdims=True)
            mx = jnp.maximum(jnp.broadcast_to(rowmax, (GROUP * BLOCK, HEAD_DIM)), sink)
            p = jnp.concatenate([jnp.exp2(t - mx) for t in sb], axis=1).astype(_BF16)
            o = jnp.dot(p, vv, preferred_element_type=_F32)
            denom = o[:, HEAD_DIM:] + jnp.exp2(sink - mx)
            o = o[:, :HEAD_DIM] * (1.0 / denom)
            for r in range(GROUP):
                cols = slice((g * GROUP + r) * HEAD_DIM, (g * GROUP + r + 1) * HEAD_DIM)
                gate = ga_ref[j * BLOCK:(j + 1) * BLOCK, cols].astype(_F32)
                o_ref[j * BLOCK:(j + 1) * BLOCK, cols] = (
                    o[r * BLOCK:(r + 1) * BLOCK, :] * gate).astype(_BF16)


def _attn(q, k, v, ga, sink, batch, seq, cast_weights=()):
    tq = TQ
    assert seq % tq == 0
    sub = tq // BLOCK
    nblk = seq // BLOCK
    nq = seq // tq
    nsteps = batch * nq
    slab = lambda b, n: (b * nq + n, 0)
    cast_specs = []
    for w in cast_weights:
        rows = w.shape[0] // nsteps
        assert w.shape[0] % nsteps == 0 and rows % 16 == 0
        cast_specs.append(pl.BlockSpec((rows, w.shape[1]), slab))
    q3 = q.reshape(batch, seq, ATTN_WIDTH)
    k3 = k.reshape(batch, seq, KV_WIDTH)
    v3 = v.reshape(batch, seq, KV_WIDTH)
    ga3 = ga.reshape(batch, seq, ATTN_WIDTH)
    cur = lambda b, n: (b, n, 0)
    prev = lambda b, n: (b, jnp.maximum(n * sub - 1, 0), 0)
    nxt = lambda b, n: (b, jnp.minimum(n * sub + sub, nblk - 1), 0)
    kv_specs = [
        pl.BlockSpec((None, BLOCK, KV_WIDTH), prev),
        pl.BlockSpec((None, tq, KV_WIDTH), cur),
        pl.BlockSpec((None, BLOCK, KV_WIDTH), nxt),
    ]
    outs = pl.pallas_call(
        functools.partial(_attn_kernel, seq, len(cast_weights)),
        grid=(batch, nq),
        in_specs=[pl.BlockSpec(memory_space=pltpu.SMEM),
                  pl.BlockSpec((None, tq, ATTN_WIDTH), cur)]
                 + kv_specs + kv_specs
                 + [pl.BlockSpec((None, tq, ATTN_WIDTH), cur)]
                 + cast_specs,
        out_specs=[pl.BlockSpec((None, tq, ATTN_WIDTH), cur)] + cast_specs,
        out_shape=[jax.ShapeDtypeStruct((batch, seq, ATTN_WIDTH), _BF16)]
                  + [jax.ShapeDtypeStruct(w.shape, _BF16) for w in cast_weights],
        compiler_params=pltpu.CompilerParams(
            dimension_semantics=("arbitrary", "arbitrary"),
            vmem_limit_bytes=VMEM_LIMIT_BYTES),
        name="window_attn",
    )(sink, q3, k3, k3, k3, v3, v3, v3, ga3, *cast_weights)
    return (outs[0].reshape(batch * seq, ATTN_WIDTH), *outs[1:])


def _out_proj_kernel(a_ref, m_ref, x_ref, p_ref, wout_ref, postg_ref, wpe_ref, wpg_ref, o_ref):
    tm = x_ref.shape[0]
    subs = [slice(t * SUB_OUT, (t + 1) * SUB_OUT) for t in range(tm // SUB_OUT)]

    def mix_proj(sub):
        am = jnp.concatenate([a_ref[sub, :], m_ref[sub, :]], axis=1)
        return jnp.dot(am, wout_ref[...], preferred_element_type=_F32)

    def post_norm(sub, y):
        ms = jnp.mean(y * y, axis=-1, keepdims=True)
        return x_ref[sub, :] + y * lax.rsqrt(ms + EPS) * postg_ref[...]

    def ple(sub, x1):
        x1b = x1.astype(_BF16)
        pb = p_ref[sub, :].astype(_BF16)
        for c0, c1 in PLE_COLUMN_PIECES:
            gate = _sigmoid(jnp.dot(x1b, wpg_ref[:, c0:c1], preferred_element_type=_F32))
            pe = jnp.dot(pb, wpe_ref[:, c0:c1], preferred_element_type=_F32)
            o_ref[sub, c0:c1] = x1[:, c0:c1] + gate * pe

    ys = [mix_proj(sub) for sub in subs]
    for sub, y in zip(subs, ys):
        ple(sub, post_norm(sub, y))


def _out_proj(a, m_gated, x2d, p2d, w_out_bf, post_g, w_pe_bf, w_pg_bf):
    m = x2d.shape[0]
    tm = TM_OUT
    assert m % tm == 0 and tm % SUB_OUT == 0
    row = lambda i: (i, 0)
    const2 = lambda i: (0, 0)
    resident = dict(pipeline_mode=pl.Buffered(1))
    return pl.pallas_call(
        _out_proj_kernel,
        grid=(m // tm,),
        in_specs=[
            pl.BlockSpec((tm, ATTN_WIDTH), row),
            pl.BlockSpec((tm, GMLP_WIDTH), row),
            pl.BlockSpec((tm, D_MODEL), row),
            pl.BlockSpec((tm, PLE_DIM), row),
            pl.BlockSpec((D_MODEL, D_MODEL), const2, **resident),
            pl.BlockSpec((1, D_MODEL), const2),
            pl.BlockSpec((PLE_DIM, D_MODEL), const2, **resident),
            pl.BlockSpec((D_MODEL, D_MODEL), const2, **resident),
        ],
        out_specs=pl.BlockSpec((tm, D_MODEL), row),
        out_shape=jax.ShapeDtypeStruct((m, D_MODEL), _F32),
        compiler_params=pltpu.CompilerParams(
            dimension_semantics=("arbitrary",),
            vmem_limit_bytes=VMEM_LIMIT_BYTES),
        name="out_proj",
    )(a, m_gated, x2d, p2d, w_out_bf, post_g, w_pe_bf, w_pg_bf)


def _rope_tables(max_seq, tm):
    inv = 1.0 / (ROPE_THETA ** (jnp.arange(0, HEAD_DIM, 2, dtype=_F32) / HEAD_DIM))
    inv2 = jnp.concatenate([inv, inv])[None, :]
    ang_row = jnp.arange(tm, dtype=_F32)[:, None] * inv2
    ang_base = jnp.arange(0, max_seq, tm, dtype=_F32)[:, None] * inv2
    nb = max_seq // tm
    sign = jnp.broadcast_to(
        jnp.where(jnp.arange(HEAD_DIM) < HEAD_DIM // 2, -1.0, 1.0).astype(_F32)[None, :],
        (nb, HEAD_DIM))
    rope_row = jnp.stack([jnp.cos(ang_row), jnp.sin(ang_row)])
    rope_base = jnp.stack([jnp.cos(ang_base), jnp.sin(ang_base), sign]
                          + [jnp.zeros((nb, HEAD_DIM), _F32)] * 5, axis=1)
    return rope_row, rope_base


def _layer(x, p, params, tables, out_weights_f32=None, out_weights_bf=None):
    (pre_g, w_in_bf, sink, ln_g, ln_b, ws_bf, bs_b, post_g) = params
    batch, seq, _ = x.shape
    rope_row, rope_base = tables
    x2d = x.reshape(batch * seq, D_MODEL)
    p2d = p.reshape(batch * seq, PLE_DIM)
    q, k, v, ga, m_gated = _in_proj(x2d, seq, pre_g, w_in_bf, rope_row, rope_base,
                                    ln_g, ln_b, ws_bf, bs_b)
    if out_weights_bf is None:
        a, *out_weights_bf = _attn(q, k, v, ga, sink, batch, seq, cast_weights=out_weights_f32)
    else:
        a, = _attn(q, k, v, ga, sink, batch, seq)
    w_out_bf, w_pg_bf, w_pe_bf = out_weights_bf
    out = _out_proj(a, m_gated, x2d, p2d, w_out_bf, post_g, w_pe_bf, w_pg_bf)
    return out.reshape(batch, seq, D_MODEL), out_weights_bf


def kernel(x_prompt, x_sample, p_prompt, p_sample, pre_norm_g, w_in, attn_sink,
           gmlp_ln_g, gmlp_ln_b, gmlp_ws, gmlp_bs, w_out, post_norm_g, w_pe, w_pg):
    depth = w_in.shape[0]
    tables = _rope_tables(max(x_prompt.shape[1], x_sample.shape[1]), TM_IN)
    y_prompt, y_sample = x_prompt, x_sample
    for i in range(depth):
        params = (
            pre_norm_g[i].reshape(1, D_MODEL),
            w_in[i].astype(_BF16),
            attn_sink[i],
            gmlp_ln_g[i].reshape(1, GMLP_WIDTH),
            gmlp_ln_b[i].reshape(1, GMLP_WIDTH),
            gmlp_ws[i].astype(_BF16),
            jnp.broadcast_to(gmlp_bs[i][:, :, None], (N_GMLP_HEADS, CHUNK, HEAD_DIM)),
            post_norm_g[i].reshape(1, D_MODEL),
        )
        y_prompt, out_weights_bf = _layer(y_prompt, p_prompt[i], params, tables,
                                          out_weights_f32=(w_out[i], w_pg[i], w_pe[i]))
        y_sample, _ = _layer(y_sample, p_sample[i], params, tables,
                             out_weights_bf=out_weights_bf)
    return (y_prompt, y_sample)
```

```python
import functools
import math

import jax
import jax.numpy as jnp
from jax import lax
from jax.experimental import pallas as pl
from jax.experimental.pallas import tpu as pltpu

D_MODEL = 2048
HEAD_DIM = 128
ATTN_WIDTH = 1024
N_HEADS = 8
N_KV = 2
GROUP = 4
KV_WIDTH = 256
WINDOW = 128
BLOCK = 128
ROPE_THETA = 10000.0
GMLP_WIDTH = 1024
N_GMLP_HEADS = 8
CHUNK = 128
PLE_DIM = 256
EPS = 1e-6
IN_WIDTH = 5632
NEG_INF = -1e30
LOG2E = math.log2(math.e)
Q_SCALE = HEAD_DIM ** -0.5 * LOG2E

VMEM_LIMIT_BYTES = 56 * 1024 * 1024

TM_IN = 512
SUB_IN = 256
NCHUNK = 512
TQ = 2048
TM_OUT = 512
SUB_OUT = 256
PLE_COLUMN_PIECES = ((0, 1024), (1024, 1792), (1792, 2048))

_BF16 = jnp.bfloat16
_F32 = jnp.float32


def _sigmoid(x):
    return 1.0 / (1.0 + jnp.exp(-x))


def _gelu_exact(x):
    return 0.5 * x * (1.0 + lax.erf(x * (2.0 ** -0.5)))


def _in_proj_kernel(x_ref, g_ref, w_ref, rope_row_ref, rope_base_ref, lng_ref, lnb_ref, ws_ref,
                    bsb_ref,
                    q_ref, k_ref, v_ref, ga_ref, m_ref, h_ref, vgf_ref, vn_ref):
    tm = x_ref.shape[0]
    rows = 64
    for r in range(tm // rows):
        x = x_ref[r * rows:(r + 1) * rows, :]
        ms = jnp.mean(x * x, axis=-1, keepdims=True)
        h = x * lax.rsqrt(ms + EPS) * g_ref[...]
        h_ref[r * rows:(r + 1) * rows, :] = h.astype(_BF16)

    cos_r, sin_r = rope_row_ref[0], rope_row_ref[1]
    base = rope_base_ref[pl.program_id(0) % rope_base_ref.shape[0]]
    cos_b, sin_b, sign = base[0:1, :], base[1:2, :], base[2:3, :]
    cos = cos_r * cos_b - sin_r * sin_b
    sin = (sin_r * cos_b + cos_r * sin_b) * sign

    def rope(t):
        return t * cos + pltpu.roll(t, HEAD_DIM // 2, 1) * sin

    def proj(start, width):
        return jnp.concatenate(
            [jnp.dot(h_ref[r * SUB_IN:(r + 1) * SUB_IN, :], w_ref[:, start:start + width],
                     preferred_element_type=_F32) for r in range(tm // SUB_IN)], axis=0)

    q0, k0, ga0 = 0, ATTN_WIDTH, ATTN_WIDTH + 2 * KV_WIDTH
    u0 = ga0 + ATTN_WIDTH
    vg0 = u0 + GMLP_WIDTH
    gg0 = vg0 + GMLP_WIDTH
    nsplit = ATTN_WIDTH // NCHUNK
    heads_per_chunk = NCHUNK // HEAD_DIM
    nchunks = tm // CHUNK
    mix_ref = vgf_ref

    def gmlp_v():
        for c in range(nsplit):
            ccols = slice(c * NCHUNK, (c + 1) * NCHUNK)
            vgf_ref[:, ccols] = _gelu_exact(proj(vg0 + c * NCHUNK, NCHUNK))

    def layer_norm():
        for r in range(tm // rows):
            rs = slice(r * rows, (r + 1) * rows)
            vf = vgf_ref[rs, :]
            mu = jnp.mean(vf, axis=-1, keepdims=True)
            vc = vf - mu
            var = jnp.mean(vc * vc, axis=-1, keepdims=True)
            vn = vc * lax.rsqrt(var + EPS) * lng_ref[...] + lnb_ref[...]
            vn_ref[rs, :] = vn.astype(_BF16)

    def spatial_mix():
        for h in range(N_GMLP_HEADS):
            cols = slice(h * HEAD_DIM, (h + 1) * HEAD_DIM)
            rhs = jnp.concatenate(
                [vn_ref[rc * CHUNK:(rc + 1) * CHUNK, cols] for rc in range(nchunks)], axis=1)
            mixed = jnp.dot(ws_ref[h], rhs, preferred_element_type=_F32)
            for rc in range(nchunks):
                mix_ref[rc * CHUNK:(rc + 1) * CHUNK, cols] = (
                    mixed[:, rc * HEAD_DIM:(rc + 1) * HEAD_DIM] + bsb_ref[h])

    def gmlp_gate():
        for c in range(nsplit):
            ccols = slice(c * NCHUNK, (c + 1) * NCHUNK)
            gu = _gelu_exact(proj(u0 + c * NCHUNK, NCHUNK))
            acc = proj(gg0 + c * NCHUNK, NCHUNK)
            m_ref[:, ccols] = (gu * mix_ref[:, ccols] * (acc * _sigmoid(acc))).astype(_BF16)

    def attn_q():
        for c in range(nsplit):
            acc = proj(q0 + c * NCHUNK, NCHUNK)
            for hd in range(heads_per_chunk):
                col = c * NCHUNK + hd * HEAD_DIM
                q_ref[:, col:col + HEAD_DIM] = (rope(
                    acc[:, hd * HEAD_DIM:(hd + 1) * HEAD_DIM]) * Q_SCALE).astype(_BF16)

    def attn_gate():
        for c in range(nsplit):
            acc = proj(ga0 + c * NCHUNK, NCHUNK)
            ga_ref[:, c * NCHUNK:(c + 1) * NCHUNK] = (acc * _sigmoid(acc)).astype(_BF16)

    def attn_kv():
        acc = proj(k0, KV_WIDTH)
        for hd in range(N_KV):
            k_ref[:, hd * HEAD_DIM:(hd + 1) * HEAD_DIM] = rope(
                acc[:, hd * HEAD_DIM:(hd + 1) * HEAD_DIM]).astype(_BF16)
        v_ref[...] = proj(k0 + KV_WIDTH, KV_WIDTH).astype(_BF16)

    gmlp_v()
    attn_q()
    layer_norm()
    spatial_mix()
    gmlp_gate()
    attn_gate()
    attn_kv()


def _in_proj(x2d, seq, pre_g, w_in_bf, rope_row, rope_base, ln_g, ln_b, ws_bf, bs_b):
    m = x2d.shape[0]
    tm = TM_IN
    assert m % tm == 0 and seq % tm == 0 and tm % CHUNK == 0
    blocks_per_seq = seq // tm
    const = lambda i: (0, 0)
    const3 = lambda i: (0, 0, 0)
    row = lambda i: (i, 0)
    resident = dict(pipeline_mode=pl.Buffered(1))
    out_widths = (ATTN_WIDTH, KV_WIDTH, KV_WIDTH, ATTN_WIDTH, GMLP_WIDTH)
    return pl.pallas_call(
        _in_proj_kernel,
        grid=(m // tm,),
        in_specs=[
            pl.BlockSpec((tm, D_MODEL), row),
            pl.BlockSpec((1, D_MODEL), const),
            pl.BlockSpec((D_MODEL, IN_WIDTH), const, **resident),
            pl.BlockSpec((2, tm, HEAD_DIM), const3),
            pl.BlockSpec((blocks_per_seq, 8, HEAD_DIM), const3),
            pl.BlockSpec((1, GMLP_WIDTH), const),
            pl.BlockSpec((1, GMLP_WIDTH), const),
            pl.BlockSpec((N_GMLP_HEADS, CHUNK, CHUNK), const3, **resident),
            pl.BlockSpec((N_GMLP_HEADS, CHUNK, HEAD_DIM), const3, **resident),
        ],
        out_specs=[pl.BlockSpec((tm, w), row) for w in out_widths],
        out_shape=[jax.ShapeDtypeStruct((m, w), _BF16) for w in out_widths],
        scratch_shapes=[pltpu.VMEM((tm, D_MODEL), _BF16),
                        pltpu.VMEM((tm, GMLP_WIDTH), _F32),
                        pltpu.VMEM((tm, GMLP_WIDTH), _BF16)],
        compiler_params=pltpu.CompilerParams(
            dimension_semantics=("arbitrary",),
            vmem_limit_bytes=VMEM_LIMIT_BYTES),
        name="in_proj",
    )(x2d, pre_g, w_in_bf, rope_row, rope_base, ln_g, ln_b, ws_bf, bs_b)


def _attn_kernel(seq, ncast, sink_ref, q_ref, kp_ref, kc_ref, kn_ref,
                 vp_ref, vc_ref, vn_ref, *rest):
    o_ref = rest[ncast]
    for src_ref, dst_ref in zip(rest[:ncast], rest[ncast + 1:]):
        dst_ref[...] = src_ref[...].astype(_BF16)

    n = pl.program_id(1)
    tq = q_ref.shape[0]
    nsub = tq // BLOCK

    ik = lax.broadcasted_iota(jnp.int32, (3 * BLOCK, BLOCK), 0)
    iq = lax.broadcasted_iota(jnp.int32, (3 * BLOCK, BLOCK), 1)
    band_t = jnp.abs(iq + BLOCK - ik) <= WINDOW
    eye = (lax.broadcasted_iota(jnp.int32, (BLOCK, BLOCK), 0)
           == lax.broadcasted_iota(jnp.int32, (BLOCK, BLOCK), 1)).astype(_BF16)
    onehot = jnp.concatenate([eye] * GROUP, axis=0)

    for j in range(nsub):
        kbase = n * tq + (j - 1) * BLOCK
        in_seq = (ik + kbase >= 0) & (ik + kbase < seq)
        bias_t = jnp.where(band_t & in_seq, 0.0, NEG_INF).astype(_BF16)
        for g in range(N_KV):
            lanes = slice(g * HEAD_DIM, (g + 1) * HEAD_DIM)

            def kv_rows(prev_ref, cur_ref, next_ref):
                lo = prev_ref[:, lanes] if j == 0 else cur_ref[(j - 1) * BLOCK:j * BLOCK, lanes]
                mid = cur_ref[j * BLOCK:(j + 1) * BLOCK, lanes]
                hi = (next_ref[:, lanes] if j == nsub - 1
                      else cur_ref[(j + 1) * BLOCK:(j + 2) * BLOCK, lanes])
                return jnp.concatenate([lo, mid, hi], axis=0)

            kk = jnp.concatenate([kv_rows(kp_ref, kc_ref, kn_ref), bias_t], axis=1)
            vv = jnp.concatenate(
                [kv_rows(vp_ref, vc_ref, vn_ref),
                 jnp.ones((3 * BLOCK, HEAD_DIM), _BF16)], axis=1)
            qs = jnp.concatenate(
                [q_ref[j * BLOCK:(j + 1) * BLOCK,
                       (g * GROUP + r) * HEAD_DIM:(g * GROUP + r + 1) * HEAD_DIM]
                 for r in range(GROUP)], axis=0)
            s = lax.dot_general(jnp.concatenate([qs, onehot], axis=1), kk,
                                (((1,), (1,)), ((), ())),
                                preferred_element_type=_F32)
            sink = jnp.concatenate(
                [jnp.full((BLOCK, HEAD_DIM), sink_ref[g * GROUP + r] * LOG2E, _F32)
                 for r in range(GROUP)], axis=0)
            sb = [s[:, i * BLOCK:(i + 1) * BLOCK] for i in range(3)]
            rowmax = jnp.max(jnp.maximum(jnp.maximum(sb[0], sb[1]), sb[2]),
                             axis=-1, keepdims=True)
            mx = jnp.maximum(jnp.broadcast_to(rowmax, (GROUP * BLOCK, HEAD_DIM)), sink)
            p = jnp.concatenate([jnp.exp2(t - mx) for t in sb], axis=1).astype(_BF16)
            o = jnp.dot(p, vv, preferred_element_type=_F32)
            denom = o[:, HEAD_DIM:] + jnp.exp2(sink - mx)
            o = o[:, :HEAD_DIM] * (1.0 / denom)
            for r in range(GROUP):
                cols = slice((g * GROUP + r) * HEAD_DIM, (g * GROUP + r + 1) * HEAD_DIM)
                o_ref[j * BLOCK:(j + 1) * BLOCK, cols] = (
                    o[r * BLOCK:(r + 1) * BLOCK, :]).astype(_BF16)


def _attn(q, k, v, sink, batch, seq, cast_weights=()):
    tq = TQ
    assert seq % tq == 0
    sub = tq // BLOCK
    nblk = seq // BLOCK
    nq = seq // tq
    nsteps = batch * nq
    slab = lambda b, n: (b * nq + n, 0)
    cast_specs = []
    for w in cast_weights:
        rows = w.shape[0] // nsteps
        assert w.shape[0] % nsteps == 0 and rows % 16 == 0
        cast_specs.append(pl.BlockSpec((rows, w.shape[1]), slab))
    q3 = q.reshape(batch, seq, ATTN_WIDTH)
    k3 = k.reshape(batch, seq, KV_WIDTH)
    v3 = v.reshape(batch, seq, KV_WIDTH)
    cur = lambda b, n: (b, n, 0)
    prev = lambda b, n: (b, jnp.maximum(n * sub - 1, 0), 0)
    nxt = lambda b, n: (b, jnp.minimum(n * sub + sub, nblk - 1), 0)
    kv_specs = [
        pl.BlockSpec((None, BLOCK, KV_WIDTH), prev),
        pl.BlockSpec((None, tq, KV_WIDTH), cur),
        pl.BlockSpec((None, BLOCK, KV_WIDTH), nxt),
    ]
    outs = pl.pallas_call(
        functools.partial(_attn_kernel, seq, len(cast_weights)),
        grid=(batch, nq),
        in_specs=[pl.BlockSpec(memory_space=pltpu.SMEM),
                  pl.BlockSpec((None, tq, ATTN_WIDTH), cur)]
                 + kv_specs + kv_specs
                 + cast_specs,
        out_specs=[pl.BlockSpec((None, tq, ATTN_WIDTH), cur)] + cast_specs,
        out_shape=[jax.ShapeDtypeStruct((batch, seq, ATTN_WIDTH), _BF16)]
                  + [jax.ShapeDtypeStruct(w.shape, _BF16) for w in cast_weights],
        compiler_params=pltpu.CompilerParams(
            dimension_semantics=("arbitrary", "arbitrary"),
            vmem_limit_bytes=VMEM_LIMIT_BYTES),
        name="window_attn",
    )(sink, q3, k3, k3, k3, v3, v3, v3, *cast_weights)
    return (outs[0].reshape(batch * seq, ATTN_WIDTH), *outs[1:])


def _out_proj_kernel(a_ref, ga_ref, m_ref, x_ref, p_ref, wout_ref, postg_ref, wpe_ref, wpg_ref,
                     o_ref):
    tm = x_ref.shape[0]
    subs = [slice(t * SUB_OUT, (t + 1) * SUB_OUT) for t in range(tm // SUB_OUT)]

    def mix_proj(sub):
        a = (a_ref[sub, :].astype(_F32) * ga_ref[sub, :].astype(_F32)).astype(_BF16)
        am = jnp.concatenate([a, m_ref[sub, :]], axis=1)
        return jnp.dot(am, wout_ref[...], preferred_element_type=_F32)

    def post_norm(sub, y):
        ms = jnp.mean(y * y, axis=-1, keepdims=True)
        return x_ref[sub, :] + y * lax.rsqrt(ms + EPS) * postg_ref[...]

    def ple(sub, x1):
        x1b = x1.astype(_BF16)
        pb = p_ref[sub, :].astype(_BF16)
        for c0, c1 in PLE_COLUMN_PIECES:
            gate = _sigmoid(jnp.dot(x1b, wpg_ref[:, c0:c1], preferred_element_type=_F32))
            pe = jnp.dot(pb, wpe_ref[:, c0:c1], preferred_element_type=_F32)
            o_ref[sub, c0:c1] = x1[:, c0:c1] + gate * pe

    ys = [mix_proj(sub) for sub in subs]
    for sub, y in zip(subs, ys):
        ple(sub, post_norm(sub, y))


def _out_proj(a, ga, m_gated, x2d, p2d, w_out_bf, post_g, w_pe_bf, w_pg_bf):
    m = x2d.shape[0]
    tm = TM_OUT
    assert m % tm == 0 and tm % SUB_OUT == 0
    row = lambda i: (i, 0)
    const2 = lambda i: (0, 0)
    resident = dict(pipeline_mode=pl.Buffered(1))
    return pl.pallas_call(
        _out_proj_kernel,
        grid=(m // tm,),
        in_specs=[
            pl.BlockSpec((tm, ATTN_WIDTH), row),
            pl.BlockSpec((tm, ATTN_WIDTH), row),
            pl.BlockSpec((tm, GMLP_WIDTH), row),
            pl.BlockSpec((tm, D_MODEL), row),
            pl.BlockSpec((tm, PLE_DIM), row),
            pl.BlockSpec((D_MODEL, D_MODEL), const2, **resident),
            pl.BlockSpec((1, D_MODEL), const2),
            pl.BlockSpec((PLE_DIM, D_MODEL), const2, **resident),
            pl.BlockSpec((D_MODEL, D_MODEL), const2, **resident),
        ],
        out_specs=pl.BlockSpec((tm, D_MODEL), row),
        out_shape=jax.ShapeDtypeStruct((m, D_MODEL), _F32),
        compiler_params=pltpu.CompilerParams(
            dimension_semantics=("arbitrary",),
            vmem_limit_bytes=VMEM_LIMIT_BYTES),
        name="out_proj",
    )(a, ga, m_gated, x2d, p2d, w_out_bf, post_g, w_pe_bf, w_pg_bf)


def _rope_tables(max_seq, tm):
    inv = 1.0 / (ROPE_THETA ** (jnp.arange(0, HEAD_DIM, 2, dtype=_F32) / HEAD_DIM))
    inv2 = jnp.concatenate([inv, inv])[None, :]
    ang_row = jnp.arange(tm, dtype=_F32)[:, None] * inv2
    ang_base = jnp.arange(0, max_seq, tm, dtype=_F32)[:, None] * inv2
    nb = max_seq // tm
    sign = jnp.broadcast_to(
        jnp.where(jnp.arange(HEAD_DIM) < HEAD_DIM // 2, -1.0, 1.0).astype(_F32)[None, :],
        (nb, HEAD_DIM))
    rope_row = jnp.stack([jnp.cos(ang_row), jnp.sin(ang_row)])
    rope_base = jnp.stack([jnp.cos(ang_base), jnp.sin(ang_base), sign]
                          + [jnp.zeros((nb, HEAD_DIM), _F32)] * 5, axis=1)
    return rope_row, rope_base


def _layer(x, p, params, tables, out_weights_f32=None, out_weights_bf=None):
    (pre_g, w_in_bf, sink, ln_g, ln_b, ws_bf, bs_b, post_g) = params
    batch, seq, _ = x.shape
    rope_row, rope_base = tables
    x2d = x.reshape(batch * seq, D_MODEL)
    p2d = p.reshape(batch * seq, PLE_DIM)
    q, k, v, ga, m_gated = _in_proj(x2d, seq, pre_g, w_in_bf, rope_row, rope_base,
                                    ln_g, ln_b, ws_bf, bs_b)
    if out_weights_bf is None:
        a, *out_weights_bf = _attn(q, k, v, sink, batch, seq, cast_weights=out_weights_f32)
    else:
        a, = _attn(q, k, v, sink, batch, seq)
    w_out_bf, w_pg_bf, w_pe_bf = out_weights_bf
    out = _out_proj(a, ga, m_gated, x2d, p2d, w_out_bf, post_g, w_pe_bf, w_pg_bf)
    return out.reshape(batch, seq, D_MODEL), out_weights_bf


def kernel(x_prompt, x_sample, p_prompt, p_sample, pre_norm_g, w_in, attn_sink,
           gmlp_ln_g, gmlp_ln_b, gmlp_ws, gmlp_bs, w_out, post_norm_g, w_pe, w_pg):
    depth = w_in.shape[0]
    tables = _rope_tables(max(x_prompt.shape[1], x_sample.shape[1]), TM_IN)
    y_prompt, y_sample = x_prompt, x_sample
    for i in range(depth):
        params = (
            pre_norm_g[i].reshape(1, D_MODEL),
            w_in[i].astype(_BF16),
            attn_sink[i],
            gmlp_ln_g[i].reshape(1, GMLP_WIDTH),
            gmlp_ln_b[i].reshape(1, GMLP_WIDTH),
            gmlp_ws[i].astype(_BF16),
            jnp.broadcast_to(gmlp_bs[i][:, :, None], (N_GMLP_HEADS, CHUNK, HEAD_DIM)),
            post_norm_g[i].reshape(1, D_MODEL),
        )
        y_prompt, out_weights_bf = _layer(y_prompt, p_prompt[i], params, tables,
                                          out_weights_f32=(w_out[i], w_pg[i], w_pe[i]))
        y_sample, _ = _layer(y_sample, p_sample[i], params, tables,
                             out_weights_bf=out_weights_bf)
    return (y_prompt, y_sample)
```
